```python
import math
import jax
import jax.numpy as jnp
from jax import lax
import numpy as np

D_MODEL = 1024
BATCH = 8
SEQ = 2048
DEPTH = 2
DEC_BATCH = 16
DEC_SEQ = 4096
PAST_LEN = 128

HEAD_DIM = 64
GRID_W = 64
NA_HEADS = 8
NA_WIN_ROWS = 8
NA_WIN_COLS = 16
NA_QCOLS = 16
NA_KCOLS = 32
GQA_Q_HEADS = 8
GQA_KV_HEADS = 2
AXIAL_THETA = 10000.0
DIFF_HEADS = 8
D_FF = 4 * D_MODEL
ROPE_THETA = 10000.0
Q_BLOCK = 128
NORM_EPS = 1e-6
QK_NORM_EPS = 1e-6
SUBLN_EPS = 1e-5

A_WIDTH = NA_HEADS * HEAD_DIM
B_Q_WIDTH = GQA_Q_HEADS * HEAD_DIM
B_KV_WIDTH = GQA_KV_HEADS * HEAD_DIM
EVEN_IN = 3 * A_WIDTH + B_Q_WIDTH + 2 * B_KV_WIDTH
EVEN_OUT = A_WIDTH + B_Q_WIDTH
EVEN_SPLITS = [A_WIDTH, 2 * A_WIDTH, 3 * A_WIDTH, 3 * A_WIDTH + B_Q_WIDTH,
               3 * A_WIDTH + B_Q_WIDTH + B_KV_WIDTH]
DIFF_WIDTH = 2 * DIFF_HEADS * HEAD_DIM
ODD_IN = 3 * DIFF_WIDTH
ODD_SPLITS = [DIFF_WIDTH, 2 * DIFF_WIDTH]
N_EVEN = (DEPTH + 1) // 2
N_ODD = DEPTH // 2

kernel_name = "hybrid_natten_gqa_diffattn_encoder"


def rmsnorm(x, g, eps=NORM_EPS):
    xf = x.astype(jnp.float32)
    y = xf * lax.rsqrt(jnp.mean(xf * xf, axis=-1, keepdims=True) + eps)
    return (y * g.astype(jnp.float32)).astype(x.dtype)


def rope_angles(pos, dim, theta):
    inv_freq = 1.0 / jnp.power(theta, jnp.arange(0, dim, 2, dtype=jnp.float32) / dim)
    ang = pos.astype(jnp.float32)[:, None] * inv_freq[None, :]
    return jnp.cos(ang), jnp.sin(ang)


def apply_rope(x, cos, sin):
    xf = x.astype(jnp.float32)
    half = xf.shape[-1] // 2
    x1, x2 = xf[..., :half], xf[..., half:]
    c = cos[None, :, None, :]
    s = sin[None, :, None, :]
    return jnp.concatenate([x1 * c - x2 * s, x2 * c + x1 * s], axis=-1).astype(x.dtype)


def apply_axial_rope(x, T):
    t = jnp.arange(T)
    half = x.shape[-1] // 2
    cr, sr = rope_angles(t // GRID_W, half, AXIAL_THETA)
    cc, sc = rope_angles(t % GRID_W, half, AXIAL_THETA)
    return jnp.concatenate([apply_rope(x[..., :half], cr, sr),
                            apply_rope(x[..., half:], cc, sc)], axis=-1)


def split_heads(z, n_heads, head_dim):
    B, T, _ = z.shape
    return z.reshape(B, T, n_heads, head_dim)


def neighbourhood_attention(q, k, v, rpb):
    B, T, H, dh = q.shape
    rows = T // GRID_W
    kr = min(NA_WIN_ROWS, rows)
    n_cb = GRID_W // NA_QCOLS
    qc = np.arange(GRID_W).reshape(n_cb, NA_QCOLS)
    band0 = np.clip(qc[:, 0] - NA_WIN_COLS // 2, 0, GRID_W - NA_KCOLS)
    kc = band0[:, None] + np.arange(NA_KCOLS)
    win0 = np.clip(qc - NA_WIN_COLS // 2, 0, GRID_W - NA_WIN_COLS)
    kcb = kc[:, None, :]
    col_mask = (kcb >= win0[..., None]) & (kcb < win0[..., None] + NA_WIN_COLS)
    dcol = np.clip(kcb - qc[..., None] + (NA_WIN_COLS - 1), 0, 2 * NA_WIN_COLS - 2)
    rpb_col = rpb[:, :, dcol]
    mask6 = jnp.asarray(col_mask)[None, None, :, :, None, :]
    qg = q.reshape(B, rows, GRID_W, H, dh)
    kg = k.reshape(B, rows, GRID_W, H, dh)
    vg = v.reshape(B, rows, GRID_W, H, dh)
    scale = dh ** -0.5

    def one_row(r):
        r0 = jnp.clip(r - kr // 2, 0, rows - kr)
        q_r = lax.dynamic_index_in_dim(qg, r, axis=1, keepdims=False).reshape(B, n_cb, NA_QCOLS, H, dh)
        k_r = lax.dynamic_slice_in_dim(kg, r0, kr, axis=1)[:, :, kc]
        v_r = lax.dynamic_slice_in_dim(vg, r0, kr, axis=1)[:, :, kc]
        s = jnp.einsum('bnqhd,brnkhd->bhnqrk', q_r, k_r).astype(jnp.float32) * scale
        drow = r0 + jnp.arange(kr) - r + (NA_WIN_ROWS - 1)
        bias = jnp.take(rpb_col, drow, axis=1).transpose(0, 2, 3, 1, 4)
        s = jnp.where(mask6, s + bias[None].astype(jnp.float32), -jnp.inf)
        p = jax.nn.softmax(s.reshape(B, H, n_cb, NA_QCOLS, kr * NA_KCOLS), axis=-1)
        p = p.reshape(B, H, n_cb, NA_QCOLS, kr, NA_KCOLS).astype(v.dtype)
        o = jnp.einsum('bhnqrk,brnkhd->bnqhd', p, v_r)
        return o.reshape(B, GRID_W, H, dh)

    out = lax.map(one_row, jnp.arange(rows))
    return out.transpose(1, 0, 2, 3, 4).reshape(B, T, H * dh)


def gqa_attention(q, k, v):
    B, T, Hq, dh = q.shape
    Hkv = k.shape[2]
    G = Hq // Hkv
    nblk = T // Q_BLOCK
    scale = dh ** -0.5
    qb = q.reshape(B, nblk, Q_BLOCK, Hkv, G, dh).transpose(1, 0, 2, 3, 4, 5)

    def one_block(q_blk):
        s = jnp.einsum('bqhgd,bkhd->bhgqk', q_blk, k).astype(jnp.float32) * scale
        p = jax.nn.softmax(s, axis=-1).astype(v.dtype)
        return jnp.einsum('bhgqk,bkhd->bqhgd', p, v)

    out = lax.map(one_block, qb)
    return out.transpose(1, 0, 2, 3, 4, 5).reshape(B, T, Hq * dh)


def diff_attention(q, k, v, lam):
    B, T, H2, dh = q.shape
    H = H2 // 2
    nblk = T // Q_BLOCK
    scale = dh ** -0.5
    qb = q.reshape(B, nblk, Q_BLOCK, H, 2, dh).transpose(1, 0, 2, 3, 4, 5)
    kk = k.reshape(B, T, H, 2, dh)

    def one_block(q_blk):
        s = jnp.einsum('bqhcd,bkhcd->bhcqk', q_blk, kk).astype(jnp.float32) * scale
        p = jax.nn.softmax(s, axis=-1)
        a = (p[:, :, 0] - lam * p[:, :, 1]).astype(v.dtype)
        return jnp.einsum('bhqk,bkhe->bqhe', a, v)

    out = lax.map(one_block, qb)
    return out.transpose(1, 0, 2, 3, 4).reshape(B, T, H, 2 * dh)


def encoder_trunk(x, ln_mix_e, w_in_e, rpb, q_norm_b, k_norm_b, w_out_e,
                  ln_mix_o, w_in_o, lambda_q1, lambda_k1, lambda_q2, lambda_k2, subln_g, w_out_o,
                  ln_mlp, w_up, w_down, ln_f):
    B, T, _ = x.shape
    cos1, sin1 = rope_angles(jnp.arange(T), HEAD_DIM, ROPE_THETA)
    for layer in range(DEPTH):
        j = layer // 2
        if layer % 2 == 0:
            h = rmsnorm(x, ln_mix_e[j])
            proj = h @ w_in_e[j]
            qa, ka, va, qb, kb, vb = jnp.split(proj, EVEN_SPLITS, axis=-1)
            a_out = neighbourhood_attention(split_heads(qa, NA_HEADS, HEAD_DIM),
                                            split_heads(ka, NA_HEADS, HEAD_DIM),
                                            split_heads(va, NA_HEADS, HEAD_DIM), rpb[j])
            qb = apply_axial_rope(rmsnorm(split_heads(qb, GQA_Q_HEADS, HEAD_DIM), q_norm_b[j], QK_NORM_EPS), T)
            kb = apply_axial_rope(rmsnorm(split_heads(kb, GQA_KV_HEADS, HEAD_DIM), k_norm_b[j], QK_NORM_EPS), T)
            b_out = gqa_attention(qb, kb, split_heads(vb, GQA_KV_HEADS, HEAD_DIM))
            x = x + jnp.concatenate([a_out, b_out], axis=-1) @ w_out_e[j]
        else:
            h = rmsnorm(x, ln_mix_o[j])
            proj = h @ w_in_o[j]
            qc, kc, vc = jnp.split(proj, ODD_SPLITS, axis=-1)
            qc = apply_rope(split_heads(qc, 2 * DIFF_HEADS, HEAD_DIM), cos1, sin1)
            kc = apply_rope(split_heads(kc, 2 * DIFF_HEADS, HEAD_DIM), cos1, sin1)
            vc = split_heads(vc, DIFF_HEADS, 2 * HEAD_DIM)
            lam_init = 0.8 - 0.6 * math.exp(-0.3 * layer)
            lam = (jnp.exp(jnp.sum(lambda_q1[j].astype(jnp.float32) * lambda_k1[j].astype(jnp.float32)))
                   - jnp.exp(jnp.sum(lambda_q2[j].astype(jnp.float32) * lambda_k2[j].astype(jnp.float32)))
                   + lam_init)
            o = diff_attention(qc, kc, vc, lam)
            o = rmsnorm(o, subln_g[j], SUBLN_EPS) * (1.0 - lam_init)
            x = x + o.reshape(B, T, DIFF_WIDTH) @ w_out_o[j]
        h = rmsnorm(x, ln_mlp[layer])
        x = x + jnp.square(jax.nn.relu(h @ w_up[layer])) @ w_down[layer]
    return rmsnorm(x, ln_f)


def setup_inputs(seed: int = 0) -> dict:
    key = jax.random.key(seed)
    ks = jax.random.split(key, 24)
    f32 = jnp.float32

    def normal(k, shape, scale):
        return jax.random.normal(k, shape, dtype=f32) * scale

    def gain(k, shape):
        return 1.0 + 0.02 * jax.random.normal(k, shape, dtype=f32)

    return {
        "x_prompt": normal(ks[0], (BATCH, SEQ, D_MODEL), 1.0),
        "x_sample": normal(ks[1], (DEC_BATCH, DEC_SEQ, D_MODEL), 1.0),
        "ln_mix_e": gain(ks[2], (N_EVEN, D_MODEL)),
        "w_in_e": normal(ks[3], (N_EVEN, D_MODEL, EVEN_IN), D_MODEL ** -0.5),
        "rpb": normal(ks[4], (N_EVEN, NA_HEADS, 2 * NA_WIN_ROWS - 1, 2 * NA_WIN_COLS - 1), 0.02),
        "q_norm_b": gain(ks[5], (N_EVEN, HEAD_DIM)),
        "k_norm_b": gain(ks[6], (N_EVEN, HEAD_DIM)),
        "w_out_e": normal(ks[7], (N_EVEN, EVEN_OUT, D_MODEL), EVEN_OUT ** -0.5),
        "ln_mix_o": gain(ks[8], (N_ODD, D_MODEL)),
        "w_in_o": normal(ks[9], (N_ODD, D_MODEL, ODD_IN), D_MODEL ** -0.5),
        "lambda_q1": normal(ks[10], (N_ODD, HEAD_DIM), 0.1),
        "lambda_k1": normal(ks[11], (N_ODD, HEAD_DIM), 0.1),
        "lambda_q2": normal(ks[12], (N_ODD, HEAD_DIM), 0.1),
        "lambda_k2": normal(ks[13], (N_ODD, HEAD_DIM), 0.1),
        "subln_g": gain(ks[14], (N_ODD, 2 * HEAD_DIM)),
        "w_out_o": normal(ks[15], (N_ODD, DIFF_WIDTH, D_MODEL), DIFF_WIDTH ** -0.5),
        "ln_mlp": gain(ks[16], (DEPTH, D_MODEL)),
        "w_up": normal(ks[17], (DEPTH, D_MODEL, D_FF), D_MODEL ** -0.5),
        "w_down": normal(ks[18], (DEPTH, D_FF, D_MODEL), D_FF ** -0.5),
        "ln_f": gain(ks[19], (D_MODEL,)),
    }


def reference(x_prompt, x_sample, ln_mix_e, w_in_e, rpb, q_norm_b, k_norm_b, w_out_e,
              ln_mix_o, w_in_o, lambda_q1, lambda_k1, lambda_q2, lambda_k2, subln_g, w_out_o,
              ln_mlp, w_up, w_down, ln_f):
    y_prompt = encoder_trunk(x_prompt, ln_mix_e, w_in_e, rpb, q_norm_b, k_norm_b, w_out_e,
                             ln_mix_o, w_in_o, lambda_q1, lambda_k1, lambda_q2, lambda_k2, subln_g, w_out_o,
                             ln_mlp, w_up, w_down, ln_f)
    y_sample = encoder_trunk(x_sample, ln_mix_e, w_in_e, rpb, q_norm_b, k_norm_b, w_out_e,
                             ln_mix_o, w_in_o, lambda_q1, lambda_k1, lambda_q2, lambda_k2, subln_g, w_out_o,
                             ln_mlp, w_up, w_down, ln_f)
    return (y_prompt, y_sample)
```

```python
import functools
import math

import numpy as np
import jax
import jax.numpy as jnp
from jax import lax
from jax.experimental import pallas as pl
from jax.experimental.pallas import tpu as pltpu

F32 = jnp.float32
BF16 = jnp.bfloat16

D_MODEL = 1024
HEAD_DIM = 64
GRID_W = 64
NA_HEADS = 8
NA_WIN_ROWS = 8
NA_WIN_COLS = 16
NA_QCOLS = 16
NA_KCOLS = 32
GQA_Q_HEADS = 8
GQA_KV_HEADS = 2
AXIAL_THETA = 10000.0
DIFF_HEADS = 8
D_FF = 4 * D_MODEL
ROPE_THETA = 10000.0
NORM_EPS = 1e-6
QK_NORM_EPS = 1e-6
SUBLN_EPS = 1e-5
A_WIDTH = NA_HEADS * HEAD_DIM
B_Q_WIDTH = GQA_Q_HEADS * HEAD_DIM
B_KV_WIDTH = GQA_KV_HEADS * HEAD_DIM
DIFF_WIDTH = 2 * DIFF_HEADS * HEAD_DIM
SM_SCALE = HEAD_DIM ** -0.5

LANES = 128
VMEM_LIMIT = 56 * 1024 * 1024

TM_PROJ = 512
TM_MLP = 1024
TF_MLP = 512
TK_ATTN = 512
ROWS_ATTN = 1024
NA_QROWS = 8
NA_KROWS = 16
NA_KPIECE = 4


def _params(*sem):
    return pltpu.CompilerParams(dimension_semantics=sem, vmem_limit_bytes=VMEM_LIMIT)


def _rmsnorm_rows(x, g, eps):
    ms = jnp.mean(x * x, axis=-1, keepdims=True)
    return x * lax.rsqrt(ms + eps) * g


def _dot(a, b):
    return jnp.dot(a, b, preferred_element_type=F32)


def _dot_nt(a, b):
    return lax.dot_general(a, b, (((1,), (1,)), ((), ())), preferred_element_type=F32)


def _lane_lo_mask():
    return lax.broadcasted_iota(jnp.int32, (1, LANES), 1) < HEAD_DIM


def _group_sumsq(x, gmat_ref):
    sq = x * x
    hi = sq.astype(BF16)
    lo = (sq - hi.astype(F32)).astype(BF16)
    return _dot(hi, gmat_ref[...]) + _dot(lo, gmat_ref[...])


def _inproj_even_kernel(x_ref, g_ref, w_ref, gmat_ref, gq_ref, gqr_ref, gk_ref, gkr_ref, cos_ref, sin_ref,
                        qa_ref, ka_ref, va_ref, qb_ref, kb_ref, vb_ref):
    h = _rmsnorm_rows(x_ref[...], g_ref[...], NORM_EPS).astype(BF16)
    a, bq, bkv = A_WIDTH, B_Q_WIDTH, B_KV_WIDTH
    qa_ref[...] = (_dot(h, w_ref[:, 0:a]) * SM_SCALE).astype(BF16)
    ka_ref[...] = _dot(h, w_ref[:, a:2 * a])
    va_ref[...] = _dot(h, w_ref[:, 2 * a:3 * a])
    o = 3 * a
    q = _dot(h, w_ref[:, o:o + bq])
    qr = _dot(h, w_ref[:, o + bq:o + 2 * bq])
    cos = cos_ref[...]
    sin = sin_ref[...]

    def norm_rope(x, xr, g, gr):
        r = lax.rsqrt(_group_sumsq(x, gmat_ref) * (1.0 / HEAD_DIM) + QK_NORM_EPS)
        return r * ((x * g) * cos + (xr * gr) * sin)

    for c in range(bq // LANES):
        sl = slice(c * LANES, (c + 1) * LANES)
        out = norm_rope(q[:, sl], qr[:, sl], gq_ref[...], gqr_ref[...])
        qb_ref[:, sl] = (out * SM_SCALE).astype(BF16)
    o += 2 * bq
    k = _dot(h, w_ref[:, o:o + bkv])
    kr = _dot(h, w_ref[:, o + bkv:o + 2 * bkv])
    kb_ref[...] = norm_rope(k, kr, gk_ref[...], gkr_ref[...]).astype(BF16)
    o += 2 * bkv
    vb_ref[...] = _dot(h, w_ref[:, o:o + bkv]).astype(BF16)


def _inproj_even(x, g, w, gmat, gq, gqr, gk, gkr, cos, sin, seq):
    n = x.shape[0]
    tm = min(TM_PROJ, seq)
    pos_blocks = seq // tm
    row = lambda i: (i, 0)
    const = lambda i: (0, 0)
    pos = lambda i: (i % pos_blocks, 0)
    wcols = w.shape[1]
    return pl.pallas_call(
        _inproj_even_kernel,
        grid=(n // tm,),
        in_specs=[
            pl.BlockSpec((tm, D_MODEL), row),
            pl.BlockSpec((1, D_MODEL), const),
            pl.BlockSpec((D_MODEL, wcols), const),
            pl.BlockSpec((LANES, LANES), const),
            pl.BlockSpec((1, LANES), const),
            pl.BlockSpec((1, LANES), const),
            pl.BlockSpec((1, LANES), const),
            pl.BlockSpec((1, LANES), const),
            pl.BlockSpec((tm, LANES), pos),
            pl.BlockSpec((tm, LANES), pos),
        ],
        out_specs=[
            pl.BlockSpec((tm, A_WIDTH), row),
            pl.BlockSpec((tm, A_WIDTH), row),
            pl.BlockSpec((tm, A_WIDTH), row),
            pl.BlockSpec((tm, B_Q_WIDTH), row),
            pl.BlockSpec((tm, B_KV_WIDTH), row),
            pl.BlockSpec((tm, B_KV_WIDTH), row),
        ],
        out_shape=[
            jax.ShapeDtypeStruct((n, A_WIDTH), BF16),
            jax.ShapeDtypeStruct((n, A_WIDTH), F32),
            jax.ShapeDtypeStruct((n, A_WIDTH), F32),
            jax.ShapeDtypeStruct((n, B_Q_WIDTH), BF16),
            jax.ShapeDtypeStruct((n, B_KV_WIDTH), BF16),
            jax.ShapeDtypeStruct((n, B_KV_WIDTH), BF16),
        ],
        compiler_params=_params("parallel"),
        name="inproj_even",
    )(x, g, w, gmat, gq, gqr, gk, gkr, cos, sin)


def _inproj_odd_kernel(x_ref, g_ref, w_ref, cos_ref, sin_ref, q_ref, k_ref, v_ref):
    h = _rmsnorm_rows(x_ref[...], g_ref[...], NORM_EPS).astype(BF16)
    d = DIFF_WIDTH
    cos = cos_ref[...]
    sin = sin_ref[...]

    def rope(o, scale, out_ref):
        x = _dot(h, w_ref[:, o:o + d])
        xr = _dot(h, w_ref[:, o + d:o + 2 * d])
        for c in range(d // LANES):
            sl = slice(c * LANES, (c + 1) * LANES)
            out_ref[:, sl] = ((x[:, sl] * cos + xr[:, sl] * sin) * scale).astype(BF16)

    rope(0, SM_SCALE, q_ref)
    rope(2 * d, 1.0, k_ref)
    v_ref[...] = _dot(h, w_ref[:, 4 * d:5 * d]).astype(BF16)


def _inproj_odd(x, g, w, cos, sin, seq):
    n = x.shape[0]
    tm = min(TM_PROJ, seq)
    pos_blocks = seq // tm
    row = lambda i: (i, 0)
    const = lambda i: (0, 0)
    pos = lambda i: (i % pos_blocks, 0)
    out = jax.ShapeDtypeStruct((n, DIFF_WIDTH), BF16)
    return pl.pallas_call(
        _inproj_odd_kernel,
        grid=(n // tm,),
        in_specs=[
            pl.BlockSpec((tm, D_MODEL), row),
            pl.BlockSpec((1, D_MODEL), const),
            pl.BlockSpec((D_MODEL, w.shape[1]), const),
            pl.BlockSpec((tm, LANES), pos),
            pl.BlockSpec((tm, LANES), pos),
        ],
        out_specs=[pl.BlockSpec((tm, DIFF_WIDTH), row)] * 3,
        out_shape=[out, out, out],
        compiler_params=_params("parallel"),
        name="inproj_odd",
    )(x, g, w, cos, sin)


def _stack_masked_heads(q_ref, qm_ref, ncols, tq):
    lo = _lane_lo_mask()
    for c in range(ncols):
        qc = q_ref[:, c * LANES:(c + 1) * LANES]
        zero = jnp.zeros_like(qc)
        qm_ref[2 * c * tq:(2 * c + 1) * tq, :] = jnp.where(lo, qc, zero)
        qm_ref[(2 * c + 1) * tq:(2 * c + 2) * tq, :] = jnp.where(lo, zero, qc)


def _softmax_pv(qm_ref, k_ref, v_ref, s_ref, m_ref, l_ref, acc_ref, nk, tk):
    m_ref[...] = jnp.full(m_ref.shape, -jnp.inf, F32)

    def scores(j, carry):
        kj = k_ref[pl.ds(pl.multiple_of(j * tk, tk), tk), :]
        s = _dot_nt(qm_ref[...], kj)
        s_ref[j] = s
        m = m_ref[...]
        for i in range(tk // LANES):
            m = jnp.maximum(m, s[:, i * LANES:(i + 1) * LANES])
        m_ref[...] = m
        return carry

    lax.fori_loop(0, nk, scores, 0)
    m_ref[...] = jnp.broadcast_to(jnp.max(m_ref[...], axis=-1, keepdims=True), m_ref.shape)
    l_ref[...] = jnp.zeros(l_ref.shape, F32)
    acc_ref[...] = jnp.zeros(acc_ref.shape, F32)

    def values(j, carry):
        s = s_ref[j]
        m = m_ref[...]
        l = l_ref[...]
        ps = []
        for i in range(tk // LANES):
            p = jnp.exp(s[:, i * LANES:(i + 1) * LANES] - m)
            l = l + p
            ps.append(p.astype(BF16))
        l_ref[...] = l
        vj = v_ref[pl.ds(pl.multiple_of(j * tk, tk), tk), :]
        acc_ref[...] += _dot(jnp.concatenate(ps, axis=1), vj)
        return carry

    lax.fori_loop(0, nk, values, 0)
    return acc_ref[...] / jnp.sum(l_ref[...], axis=-1, keepdims=True)


def _gqa_kernel(q_ref, k_ref, v_ref, o_ref, qm_ref, s_ref, m_ref, l_ref, acc_ref, *, tq, nk, tk):
    ncols = B_Q_WIDTH // LANES
    _stack_masked_heads(q_ref, qm_ref, ncols, tq)
    o = _softmax_pv(qm_ref, k_ref, v_ref, s_ref, m_ref, l_ref, acc_ref, nk, tk)
    lo = _lane_lo_mask()
    for c in range(ncols):
        o_lo = o[2 * c * tq:(2 * c + 1) * tq]
        o_hi = o[(2 * c + 1) * tq:(2 * c + 2) * tq]
        o_ref[:, c * LANES:(c + 1) * LANES] = jnp.where(lo, o_lo, o_hi).astype(BF16)


def _gqa_attention(q, k, v, batch, seq):
    ncols = B_Q_WIDTH // LANES
    tq = min(ROWS_ATTN // (2 * ncols), seq)
    tk = min(TK_ATTN, seq)
    nk = seq // tk
    rows = 2 * ncols * tq
    nq = seq // tq
    return pl.pallas_call(
        functools.partial(_gqa_kernel, tq=tq, nk=nk, tk=tk),
        grid=(batch, nq),
        in_specs=[
            pl.BlockSpec((tq, B_Q_WIDTH), lambda b, i: (b * nq + i, 0)),
            pl.BlockSpec((seq, B_KV_WIDTH), lambda b, i: (b, 0)),
            pl.BlockSpec((seq, B_KV_WIDTH), lambda b, i: (b, 0)),
        ],
        out_specs=pl.BlockSpec((tq, B_Q_WIDTH), lambda b, i: (b * nq + i, 0)),
        out_shape=jax.ShapeDtypeStruct(q.shape, BF16),
        scratch_shapes=[
            pltpu.VMEM((rows, LANES), BF16),
            pltpu.VMEM((nk, rows, tk), F32),
            pltpu.VMEM((rows, LANES), F32),
            pltpu.VMEM((rows, LANES), F32),
            pltpu.VMEM((rows, LANES), F32),
        ],
        compiler_params=_params("parallel", "parallel"),
        name="gqa_attention",
    )(q, k, v)


def _diff_kernel(q_ref, k_ref, v_ref, lq1_ref, lk1_ref, lq2_ref, lk2_ref, g_ref, o_ref,
                 qm_ref, s_ref, m_ref, l_ref, acc_ref, *, tq, nk, tk, lam_init):
    _stack_masked_heads(q_ref, qm_ref, 1, tq)
    o = _softmax_pv(qm_ref, k_ref, v_ref, s_ref, m_ref, l_ref, acc_ref, nk, tk)
    lam = (jnp.exp(jnp.sum(lq1_ref[...] * lk1_ref[...], axis=-1, keepdims=True))
           - jnp.exp(jnp.sum(lq2_ref[...] * lk2_ref[...], axis=-1, keepdims=True)) + lam_init)
    d = o[0:tq] - lam * o[tq:2 * tq]
    o_ref[...] = (_rmsnorm_rows(d, g_ref[...], SUBLN_EPS) * (1.0 - lam_init)).astype(BF16)


def _diff_attention(q, k, v, lq1, lk1, lq2, lk2, g, batch, seq, lam_init):
    tq = min(ROWS_ATTN // 2, seq)
    tk = min(TK_ATTN, seq)
    nk = seq // tk
    rows = 2 * tq
    nq = seq // tq
    vec = pl.BlockSpec((1, HEAD_DIM), lambda b, h, i: (0, 0))
    return pl.pallas_call(
        functools.partial(_diff_kernel, tq=tq, nk=nk, tk=tk, lam_init=lam_init),
        grid=(batch, DIFF_HEADS, nq),
        in_specs=[
            pl.BlockSpec((tq, LANES), lambda b, h, i: (b * nq + i, h)),
            pl.BlockSpec((seq, LANES), lambda b, h, i: (b, h)),
            pl.BlockSpec((seq, LANES), lambda b, h, i: (b, h)),
            vec, vec, vec, vec,
            pl.BlockSpec((1, LANES), lambda b, h, i: (0, 0)),
        ],
        out_specs=pl.BlockSpec((tq, LANES), lambda b, h, i: (b * nq + i, h)),
        out_shape=jax.ShapeDtypeStruct(q.shape, BF16),
        scratch_shapes=[
            pltpu.VMEM((rows, LANES), BF16),
            pltpu.VMEM((nk, rows, tk), F32),
            pltpu.VMEM((rows, LANES), F32),
            pltpu.VMEM((rows, LANES), F32),
            pltpu.VMEM((rows, LANES), F32),
        ],
        compiler_params=_params("parallel", "parallel", "parallel"),
        name="diff_attention",
    )(q, k, v, lq1, lk1, lq2, lk2, g)


def _na_band_start(n):
    return int(np.clip(n * NA_QCOLS - NA_WIN_COLS // 2, 0, GRID_W - NA_KCOLS))


def _na_bias_indices():
    nq = NA_QROWS * NA_QCOLS
    nkeys = NA_KROWS * NA_KCOLS
    qr = (np.arange(nq) // NA_QCOLS)[:, None]
    qc = (np.arange(nq) % NA_QCOLS)[:, None]
    kr = (np.arange(nkeys) // NA_KCOLS)[None, :]
    kc = (np.arange(nkeys) % NA_KCOLS)[None, :]
    half = NA_WIN_ROWS // 2
    key_shift = [0, -half, -(NA_KROWS - NA_QROWS)]
    win_start = [np.maximum(qr - half, 0), qr, np.minimum(qr + half, NA_KROWS - NA_WIN_ROWS)]
    drow, rmask = [], []
    for shift, w0 in zip(key_shift, win_start):
        drow.append(np.clip(kr + shift - qr + NA_WIN_ROWS - 1, 0, 2 * NA_WIN_ROWS - 2) + 0 * qc)
        rmask.append(((kr >= w0) & (kr < w0 + NA_WIN_ROWS)) | (qc < 0))
    dcol, cmask = [], []
    for n in range(GRID_W // NA_QCOLS):
        qabs = n * NA_QCOLS + qc
        kabs = _na_band_start(n) + kc
        w0 = np.clip(qabs - NA_WIN_COLS // 2, 0, GRID_W - NA_WIN_COLS)
        dcol.append(np.clip(kabs - qabs + NA_WIN_COLS - 1, 0, 2 * NA_WIN_COLS - 2) + 0 * qr)
        cmask.append(((kabs >= w0) & (kabs < w0 + NA_WIN_COLS)) | (qr < 0))
    drow = np.stack(drow)[:, None]
    dcol = np.stack(dcol)[None, :]
    mask = np.stack(rmask)[:, None] & np.stack(cmask)[None, :]
    shape = mask.shape
    return np.broadcast_to(drow, shape), np.broadcast_to(dcol, shape), mask


def _na_bias_table(rpb):
    drow, dcol, mask = _na_bias_indices()
    tbl = rpb.astype(F32)[:, drow, dcol]
    tbl = jnp.where(jnp.asarray(mask)[None], tbl, -jnp.inf)
    return tbl.transpose(1, 2, 0, 3, 4)


def _na_kernel(q_ref, k0, k1, k2, k3, v0, v1, v2, v3, tbl_ref, o_ref):
    kps = (k0, k1, k2, k3)
    vps = (v0, v1, v2, v3)
    lo = _lane_lo_mask()
    nq = NA_QROWS * NA_QCOLS
    for n in range(GRID_W // NA_QCOLS):
        band = _na_band_start(n)
        for c in range(A_WIDTH // LANES):
            lanes = slice(c * LANES, (c + 1) * LANES)
            qs = jnp.concatenate(
                [q_ref[r * GRID_W + n * NA_QCOLS:r * GRID_W + (n + 1) * NA_QCOLS, lanes] for r in range(NA_QROWS)],
                axis=0)
            zero = jnp.zeros_like(qs)
            qm = jnp.concatenate([jnp.where(lo, qs, zero), jnp.where(lo, zero, qs)], axis=0)

            def band_rows(pieces):
                return jnp.concatenate(
                    [p[r * GRID_W + band:r * GRID_W + band + NA_KCOLS, lanes]
                     for p in pieces for r in range(NA_KPIECE)], axis=0).astype(BF16)

            kb = band_rows(kps)
            vb = band_rows(vps)
            s = _dot_nt(qm, kb)
            s = s + jnp.concatenate([tbl_ref[0, n, 2 * c], tbl_ref[0, n, 2 * c + 1]], axis=0)
            m = jnp.max(s, axis=-1, keepdims=True)
            p = jnp.exp(s - m)
            l = jnp.sum(p, axis=-1, keepdims=True)
            o = _dot(p.astype(BF16), vb) / l
            res = jnp.where(lo, o[0:nq], o[nq:2 * nq]).astype(BF16)
            for r in range(NA_QROWS):
                o_ref[r * GRID_W + n * NA_QCOLS:r * GRID_W + (n + 1) * NA_QCOLS, lanes] = (
                    res[r * NA_QCOLS:(r + 1) * NA_QCOLS])


def _na_attention(q, k, v, tbl, batch, seq):
    grid_rows = seq // GRID_W
    steps = grid_rows // NA_QROWS
    tq = NA_QROWS * GRID_W
    tp = NA_KPIECE * GRID_W
    pieces_per_batch = grid_rows // NA_KPIECE
    npieces = NA_KROWS // NA_KPIECE
    shift = (NA_WIN_ROWS // 2) // NA_KPIECE

    def piece_map(i):
        def index(b, t):
            start = jnp.clip(t * (NA_QROWS // NA_KPIECE) - shift, 0, pieces_per_batch - npieces)
            return (b * pieces_per_batch + start + i, 0)
        return index

    def tbl_map(b, t):
        variant = jnp.where(t == 0, 0, jnp.where(t == steps - 1, 2, 1))
        return (variant, 0, 0, 0, 0)

    piece_specs = [pl.BlockSpec((tp, A_WIDTH), piece_map(i)) for i in range(npieces)]
    return pl.pallas_call(
        _na_kernel,
        grid=(batch, steps),
        in_specs=[pl.BlockSpec((tq, A_WIDTH), lambda b, t: (b * steps + t, 0))] + piece_specs + piece_specs + [
            pl.BlockSpec((1,) + tbl.shape[1:], tbl_map)],
        out_specs=pl.BlockSpec((tq, A_WIDTH), lambda b, t: (b * steps + t, 0)),
        out_shape=jax.ShapeDtypeStruct(q.shape, BF16),
        compiler_params=_params("parallel", "arbitrary"),
        name="na_attention",
    )(q, k, k, k, k, v, v, v, v, tbl)


def _mlp_kernel(*refs, n_attn, final):
    x_ref = refs[0]
    attn_refs = refs[1:1 + n_attn]
    wo_ref, g_ref, wup_ref, wdown_ref, gf_ref, o_ref, h_ref, acc_ref = refs[1 + n_attn:]
    j = pl.program_id(1)

    @pl.when(j == 0)
    def _():
        x = x_ref[...]
        width = wo_ref.shape[0] // n_attn
        for i, a_ref in enumerate(attn_refs):
            x = x + _dot(a_ref[...], wo_ref[i * width:(i + 1) * width, :])
        acc_ref[...] = x
        h_ref[...] = _rmsnorm_rows(x, g_ref[...], NORM_EPS).astype(BF16)

    u = jnp.maximum(_dot(h_ref[...], wup_ref[...]), 0.0)
    acc_ref[...] += _dot((u * u).astype(BF16), wdown_ref[...])

    @pl.when(j == pl.num_programs(1) - 1)
    def _():
        y = acc_ref[...]
        if final:
            y = _rmsnorm_rows(y, gf_ref[...], NORM_EPS)
        o_ref[...] = y


def _outproj_mlp(x, attn, wo, g, wup, wdown, gf, final):
    n = x.shape[0]
    tm = min(TM_MLP, n)
    tf = TF_MLP
    row = lambda i, j: (i, 0)
    const = lambda i, j: (0, 0)
    return pl.pallas_call(
        functools.partial(_mlp_kernel, n_attn=len(attn), final=final),
        grid=(n // tm, D_FF // tf),
        in_specs=[pl.BlockSpec((tm, D_MODEL), row)]
        + [pl.BlockSpec((tm, a.shape[1]), row) for a in attn]
        + [
            pl.BlockSpec(wo.shape, const),
            pl.BlockSpec((1, D_MODEL), const),
            pl.BlockSpec((D_MODEL, tf), lambda i, j: (0, j)),
            pl.BlockSpec((tf, D_MODEL), lambda i, j: (j, 0)),
            pl.BlockSpec((1, D_MODEL), const),
        ],
        out_specs=pl.BlockSpec((tm, D_MODEL), row),
        out_shape=jax.ShapeDtypeStruct(x.shape, F32),
        scratch_shapes=[pltpu.VMEM((tm, D_MODEL), BF16), pltpu.VMEM((tm, D_MODEL), F32)],
        compiler_params=_params("parallel", "arbitrary"),
        name="outproj_mlp",
    )(x, *attn, wo, g, wup, wdown, gf)


def _rope_angles(pos, dim, theta):
    inv_freq = 1.0 / jnp.power(theta, jnp.arange(0, dim, 2, dtype=F32) / dim)
    ang = pos.astype(F32)[:, None] * inv_freq[None, :]
    return jnp.cos(ang), jnp.sin(ang)


def _axial_tables(seq):
    t = jnp.arange(seq)
    half = HEAD_DIM // 2
    cr, sr = _rope_angles(t // GRID_W, half, AXIAL_THETA)
    cc, sc = _rope_angles(t % GRID_W, half, AXIAL_THETA)
    cos = jnp.concatenate([cr, cr, cc, cc], axis=-1)
    sin = jnp.concatenate([sr, sr, sc, sc], axis=-1)
    return jnp.tile(cos, (1, LANES // HEAD_DIM)), jnp.tile(sin, (1, LANES // HEAD_DIM))


def _rope_tables(seq):
    c, s = _rope_angles(jnp.arange(seq), HEAD_DIM, ROPE_THETA)
    cos = jnp.concatenate([c, c], axis=-1)
    sin = jnp.concatenate([s, s], axis=-1)
    return jnp.tile(cos, (1, LANES // HEAD_DIM)), jnp.tile(sin, (1, LANES // HEAD_DIM))


_q = HEAD_DIM // 4
_AXIAL_SRC = np.concatenate([np.arange(_q, 2 * _q), np.arange(0, _q), np.arange(3 * _q, 4 * _q), np.arange(2 * _q, 3 * _q)])
_AXIAL_SIGN = np.concatenate([-np.ones(_q), np.ones(_q), -np.ones(_q), np.ones(_q)]).astype(np.float32)
_h = HEAD_DIM // 2
_ROPE_SRC = np.concatenate([np.arange(_h, 2 * _h), np.arange(0, _h)])
_ROPE_SIGN = np.concatenate([-np.ones(_h), np.ones(_h)]).astype(np.float32)
_GQA_HEAD_ORDER = np.arange(GQA_Q_HEADS).reshape(GQA_KV_HEADS, -1).T.reshape(-1)


def _rotated_columns(w, heads, src, sign):
    w3 = w.reshape(w.shape[0], heads, HEAD_DIM)
    return (w3[:, :, src] * sign).reshape(w.shape)


def _head_columns(w, order):
    w3 = w.reshape(w.shape[0], -1, HEAD_DIM)
    return w3[:, order, :].reshape(w.shape)


def _prepare_even(w_in, q_norm, k_norm, w_out):
    a, bq, bkv = A_WIDTH, B_Q_WIDTH, B_KV_WIDTH
    wqa, wka, wva = w_in[:, 0:a], w_in[:, a:2 * a], w_in[:, 2 * a:3 * a]
    wqb = _head_columns(w_in[:, 3 * a:3 * a + bq], _GQA_HEAD_ORDER)
    wkb = w_in[:, 3 * a + bq:3 * a + bq + bkv]
    wvb = w_in[:, 3 * a + bq + bkv:]
    w = jnp.concatenate([
        wqa, wka, wva,
        wqb, _rotated_columns(wqb, GQA_Q_HEADS, _AXIAL_SRC, _AXIAL_SIGN),
        wkb, _rotated_columns(wkb, GQA_KV_HEADS, _AXIAL_SRC, _AXIAL_SIGN),
        wvb], axis=1).astype(BF16)
    reps = LANES // HEAD_DIM
    gq = jnp.tile(q_norm, reps)[None]
    gqr = jnp.tile(q_norm[_AXIAL_SRC], reps)[None]
    gk = jnp.tile(k_norm, reps)[None]
    gkr = jnp.tile(k_norm[_AXIAL_SRC], reps)[None]
    wo_b = w_out[a:].reshape(GQA_Q_HEADS, HEAD_DIM, -1)[_GQA_HEAD_ORDER].reshape(bq, -1)
    wo = jnp.concatenate([w_out[:a], wo_b], axis=0).astype(BF16)
    return w, gq, gqr, gk, gkr, wo


def _prepare_odd(w_in):
    d = DIFF_WIDTH
    wq, wk, wv = w_in[:, 0:d], w_in[:, d:2 * d], w_in[:, 2 * d:]
    heads = d // HEAD_DIM
    return jnp.concatenate([
        wq, _rotated_columns(wq, heads, _ROPE_SRC, _ROPE_SIGN),
        wk, _rotated_columns(wk, heads, _ROPE_SRC, _ROPE_SIGN),
        wv], axis=1).astype(BF16)


def _group_ones():
    g = np.arange(LANES) // HEAD_DIM
    return jnp.asarray((g[:, None] == g[None, :]).astype(np.float32), dtype=BF16)


def _trunk(x3, p):
    batch, seq, _ = x3.shape
    assert seq % (NA_KROWS * GRID_W) == 0 and seq % TK_ATTN == 0
    x = x3.reshape(batch * seq, D_MODEL)
    depth = len(p["layers"])
    for layer, lp in enumerate(p["layers"]):
        final = layer == depth - 1
        if layer % 2 == 0:
            cos, sin = _axial_tables(seq)
            qa, ka, va, qb, kb, vb = _inproj_even(x, lp["ln_mix"], lp["w_in"], p["gmat"], lp["gq"], lp["gqr"],
                                                  lp["gk"], lp["gkr"], cos, sin, seq)
            a_out = _na_attention(qa, ka, va, lp["na_bias"], batch, seq)
            b_out = _gqa_attention(qb, kb, vb, batch, seq)
            attn = [a_out, b_out]
        else:
            cos, sin = _rope_tables(seq)
            q, k, v = _inproj_odd(x, lp["ln_mix"], lp["w_in"], cos, sin, seq)
            attn = [_diff_attention(q, k, v, lp["lq1"], lp["lk1"], lp["lq2"], lp["lk2"], lp["subln"],
                                    batch, seq, lp["lam_init"])]
        x = _outproj_mlp(x, attn, lp["w_out"], lp["ln_mlp"], lp["w_up"], lp["w_down"], p["ln_f"], final)
    return x.reshape(batch, seq, D_MODEL)


def kernel(x_prompt, x_sample, ln_mix_e, w_in_e, rpb, q_norm_b, k_norm_b, w_out_e, ln_mix_o, w_in_o, lambda_q1, lambda_k1, lambda_q2, lambda_k2, subln_g, w_out_o, ln_mlp, w_up, w_down, ln_f):
    depth = ln_mlp.shape[0]
    layers = []
    for layer in range(depth):
        j = layer // 2
        lp = {
            "ln_mlp": ln_mlp[layer][None],
            "w_up": w_up[layer].astype(BF16),
            "w_down": w_down[layer].astype(BF16),
        }
        if layer % 2 == 0:
            w, gq, gqr, gk, gkr, wo = _prepare_even(w_in_e[j], q_norm_b[j], k_norm_b[j], w_out_e[j])
            lp.update(ln_mix=ln_mix_e[j][None], w_in=w, gq=gq, gqr=gqr, gk=gk, gkr=gkr, w_out=wo,
                      na_bias=_na_bias_table(rpb[j]))
        else:
            lp.update(ln_mix=ln_mix_o[j][None], w_in=_prepare_odd(w_in_o[j]), w_out=w_out_o[j].astype(BF16),
                      lq1=lambda_q1[j][None], lk1=lambda_k1[j][None], lq2=lambda_q2[j][None],
                      lk2=lambda_k2[j][None], subln=subln_g[j][None],
                      lam_init=0.8 - 0.6 * math.exp(-0.3 * layer))
        layers.append(lp)
    p = {"layers": layers, "gmat": _group_ones(), "ln_f": ln_f[None]}
    return (_trunk(x_prompt, p), _trunk(x_sample, p))
```

```python
import functools
import math

import numpy as np
import jax
import jax.numpy as jnp
from jax import lax
from jax.experimental import pallas as pl
from jax.experimental.pallas import tpu as pltpu

F32 = jnp.float32
BF16 = jnp.bfloat16

D_MODEL = 1024
HEAD_DIM = 64
GRID_W = 64
NA_HEADS = 8
NA_WIN_ROWS = 8
NA_WIN_COLS = 16
NA_QCOLS = 16
NA_KCOLS = 32
GQA_Q_HEADS = 8
GQA_KV_HEADS = 2
AXIAL_THETA = 10000.0
DIFF_HEADS = 8
D_FF = 4 * D_MODEL
ROPE_THETA = 10000.0
NORM_EPS = 1e-6
QK_NORM_EPS = 1e-6
SUBLN_EPS = 1e-5
A_WIDTH = NA_HEADS * HEAD_DIM
B_Q_WIDTH = GQA_Q_HEADS * HEAD_DIM
B_KV_WIDTH = GQA_KV_HEADS * HEAD_DIM
DIFF_WIDTH = 2 * DIFF_HEADS * HEAD_DIM
SM_SCALE = HEAD_DIM ** -0.5
LOG2E = math.log2(math.e)
Q_SCALE = SM_SCALE * LOG2E

LANES = 128
VMEM_LIMIT = 56 * 1024 * 1024

TM_PROJ = 512
TM_MLP = 1024
TF_MLP = 512
TK_ATTN = 512
ROWS_ATTN = 1024
NA_QROWS = 8
NA_KROWS = 16
NA_KPIECE = 4


def _params(*sem):
    return pltpu.CompilerParams(dimension_semantics=sem, vmem_limit_bytes=VMEM_LIMIT)


def _rmsnorm_rows(x, g, eps):
    ms = jnp.mean(x * x, axis=-1, keepdims=True)
    return x * lax.rsqrt(ms + eps) * g


def _dot(a, b):
    return jnp.dot(a, b, preferred_element_type=F32)


def _dot_nt(a, b):
    return lax.dot_general(a, b, (((1,), (1,)), ((), ())), preferred_element_type=F32)


def _lane_lo_mask():
    return lax.broadcasted_iota(jnp.int32, (1, LANES), 1) < HEAD_DIM


def _group_sumsq(x, gmat_ref):
    sq = x * x
    hi = sq.astype(BF16)
    lo = (sq - hi.astype(F32)).astype(BF16)
    return _dot(hi, gmat_ref[...]) + _dot(lo, gmat_ref[...])


def _inproj_even_kernel(x_ref, g_ref, w_ref, gmat_ref, gq_ref, gqr_ref, gk_ref, gkr_ref, cos_ref, sin_ref,
                        qa_ref, ka_ref, va_ref, qb_ref, kb_ref, vb_ref):
    h = _rmsnorm_rows(x_ref[...], g_ref[...], NORM_EPS).astype(BF16)
    a, bq, bkv = A_WIDTH, B_Q_WIDTH, B_KV_WIDTH
    qa_ref[...] = (_dot(h, w_ref[:, 0:a]) * Q_SCALE).astype(BF16)
    ka_ref[...] = _dot(h, w_ref[:, a:2 * a])
    va_ref[...] = _dot(h, w_ref[:, 2 * a:3 * a])
    o = 3 * a
    q = _dot(h, w_ref[:, o:o + bq])
    qr = _dot(h, w_ref[:, o + bq:o + 2 * bq])
    cos = cos_ref[...]
    sin = sin_ref[...]

    def norm_rope(x, xr, g, gr):
        r = lax.rsqrt(_group_sumsq(x, gmat_ref) * (1.0 / HEAD_DIM) + QK_NORM_EPS)
        return r * ((x * g) * cos + (xr * gr) * sin)

    for c in range(bq // LANES):
        sl = slice(c * LANES, (c + 1) * LANES)
        out = norm_rope(q[:, sl], qr[:, sl], gq_ref[...], gqr_ref[...])
        qb_ref[:, sl] = (out * Q_SCALE).astype(BF16)
    o += 2 * bq
    k = _dot(h, w_ref[:, o:o + bkv])
    kr = _dot(h, w_ref[:, o + bkv:o + 2 * bkv])
    kb_ref[...] = norm_rope(k, kr, gk_ref[...], gkr_ref[...]).astype(BF16)
    o += 2 * bkv
    vb_ref[...] = _dot(h, w_ref[:, o:o + bkv]).astype(BF16)


def _inproj_even(x, g, w, gmat, gq, gqr, gk, gkr, cos, sin, seq):
    n = x.shape[0]
    tm = min(TM_PROJ, seq)
    pos_blocks = seq // tm
    row = lambda i: (i, 0)
    const = lambda i: (0, 0)
    pos = lambda i: (i % pos_blocks, 0)
    wcols = w.shape[1]
    return pl.pallas_call(
        _inproj_even_kernel,
        grid=(n // tm,),
        in_specs=[
            pl.BlockSpec((tm, D_MODEL), row),
            pl.BlockSpec((1, D_MODEL), const),
            pl.BlockSpec((D_MODEL, wcols), const),
            pl.BlockSpec((LANES, LANES), const),
            pl.BlockSpec((1, LANES), const),
            pl.BlockSpec((1, LANES), const),
            pl.BlockSpec((1, LANES), const),
            pl.BlockSpec((1, LANES), const),
            pl.BlockSpec((tm, LANES), pos),
            pl.BlockSpec((tm, LANES), pos),
        ],
        out_specs=[
            pl.BlockSpec((tm, A_WIDTH), row),
            pl.BlockSpec((tm, A_WIDTH), row),
            pl.BlockSpec((tm, A_WIDTH), row),
            pl.BlockSpec((tm, B_Q_WIDTH), row),
            pl.BlockSpec((tm, B_KV_WIDTH), row),
            pl.BlockSpec((tm, B_KV_WIDTH), row),
        ],
        out_shape=[
            jax.ShapeDtypeStruct((n, A_WIDTH), BF16),
            jax.ShapeDtypeStruct((n, A_WIDTH), F32),
            jax.ShapeDtypeStruct((n, A_WIDTH), F32),
            jax.ShapeDtypeStruct((n, B_Q_WIDTH), BF16),
            jax.ShapeDtypeStruct((n, B_KV_WIDTH), BF16),
            jax.ShapeDtypeStruct((n, B_KV_WIDTH), BF16),
        ],
        compiler_params=_params("parallel"),
        name="inproj_even",
    )(x, g, w, gmat, gq, gqr, gk, gkr, cos, sin)


def _inproj_odd_kernel(x_ref, g_ref, w_ref, cos_ref, sin_ref, q_ref, k_ref, v_ref):
    h = _rmsnorm_rows(x_ref[...], g_ref[...], NORM_EPS).astype(BF16)
    d = DIFF_WIDTH
    cos = cos_ref[...]
    sin = sin_ref[...]

    def rope(o, scale, out_ref):
        x = _dot(h, w_ref[:, o:o + d])
        xr = _dot(h, w_ref[:, o + d:o + 2 * d])
        for c in range(d // LANES):
            sl = slice(c * LANES, (c + 1) * LANES)
            out_ref[:, sl] = ((x[:, sl] * cos + xr[:, sl] * sin) * scale).astype(BF16)

    rope(0, Q_SCALE, q_ref)
    rope(2 * d, 1.0, k_ref)
    v_ref[...] = _dot(h, w_ref[:, 4 * d:5 * d]).astype(BF16)


def _inproj_odd(x, g, w, cos, sin, seq):
    n = x.shape[0]
    tm = min(TM_PROJ, seq)
    pos_blocks = seq // tm
    row = lambda i: (i, 0)
    const = lambda i: (0, 0)
    pos = lambda i: (i % pos_blocks, 0)
    out = jax.ShapeDtypeStruct((n, DIFF_WIDTH), BF16)
    return pl.pallas_call(
        _inproj_odd_kernel,
        grid=(n // tm,),
        in_specs=[
            pl.BlockSpec((tm, D_MODEL), row),
            pl.BlockSpec((1, D_MODEL), const),
            pl.BlockSpec((D_MODEL, w.shape[1]), const),
            pl.BlockSpec((tm, LANES), pos),
            pl.BlockSpec((tm, LANES), pos),
        ],
        out_specs=[pl.BlockSpec((tm, DIFF_WIDTH), row)] * 3,
        out_shape=[out, out, out],
        compiler_params=_params("parallel"),
        name="inproj_odd",
    )(x, g, w, cos, sin)


def _stack_masked_heads(q_ref, qm_ref, ncols, tq):
    lo = _lane_lo_mask()
    for c in range(ncols):
        qc = q_ref[:, c * LANES:(c + 1) * LANES]
        zero = jnp.zeros_like(qc)
        qm_ref[2 * c * tq:(2 * c + 1) * tq, :] = jnp.where(lo, qc, zero)
        qm_ref[(2 * c + 1) * tq:(2 * c + 2) * tq, :] = jnp.where(lo, zero, qc)


def _softmax_pv(qm_ref, k_ref, v_ref, s_ref, m_ref, l_ref, acc_ref, nk, tk):
    m_ref[...] = jnp.full(m_ref.shape, -jnp.inf, F32)

    def scores(j, carry):
        kj = k_ref[pl.ds(pl.multiple_of(j * tk, tk), tk), :]
        s = _dot_nt(qm_ref[...], kj)
        s_ref[j] = s
        m = m_ref[...]
        for i in range(tk // LANES):
            m = jnp.maximum(m, s[:, i * LANES:(i + 1) * LANES])
        m_ref[...] = m
        return carry

    lax.fori_loop(0, nk, scores, 0)
    m_ref[...] = jnp.broadcast_to(jnp.max(m_ref[...], axis=-1, keepdims=True), m_ref.shape)
    l_ref[...] = jnp.zeros(l_ref.shape, F32)
    acc_ref[...] = jnp.zeros(acc_ref.shape, F32)

    def values(j, carry):
        s = s_ref[j]
        m = m_ref[...]
        l = l_ref[...]
        ps = []
        for i in range(tk // LANES):
            p = jnp.exp2(s[:, i * LANES:(i + 1) * LANES] - m)
            l = l + p
            ps.append(p.astype(BF16))
        l_ref[...] = l
        vj = v_ref[pl.ds(pl.multiple_of(j * tk, tk), tk), :]
        acc_ref[...] += _dot(jnp.concatenate(ps, axis=1), vj)
        return carry

    lax.fori_loop(0, nk, values, 0)
    return acc_ref[...] / jnp.sum(l_ref[...], axis=-1, keepdims=True)


def _gqa_kernel(q_ref, k_ref, v_ref, o_ref, qm_ref, s_ref, m_ref, l_ref, acc_ref, *, tq, nk, tk):
    ncols = B_Q_WIDTH // LANES
    _stack_masked_heads(q_ref, qm_ref, ncols, tq)
    o = _softmax_pv(qm_ref, k_ref, v_ref, s_ref, m_ref, l_ref, acc_ref, nk, tk)
    lo = _lane_lo_mask()
    for c in range(ncols):
        o_lo = o[2 * c * tq:(2 * c + 1) * tq]
        o_hi = o[(2 * c + 1) * tq:(2 * c + 2) * tq]
        o_ref[:, c * LANES:(c + 1) * LANES] = jnp.where(lo, o_lo, o_hi).astype(BF16)


def _gqa_attention(q, k, v, batch, seq):
    ncols = B_Q_WIDTH // LANES
    tq = min(ROWS_ATTN // (2 * ncols), seq)
    tk = min(TK_ATTN, seq)
    nk = seq // tk
    rows = 2 * ncols * tq
    nq = seq // tq
    return pl.pallas_call(
        functools.partial(_gqa_kernel, tq=tq, nk=nk, tk=tk),
        grid=(batch, nq),
        in_specs=[
            pl.BlockSpec((tq, B_Q_WIDTH), lambda b, i: (b * nq + i, 0)),
            pl.BlockSpec((seq, B_KV_WIDTH), lambda b, i: (b, 0)),
            pl.BlockSpec((seq, B_KV_WIDTH), lambda b, i: (b, 0)),
        ],
        out_specs=pl.BlockSpec((tq, B_Q_WIDTH), lambda b, i: (b * nq + i, 0)),
        out_shape=jax.ShapeDtypeStruct(q.shape, BF16),
        scratch_shapes=[
            pltpu.VMEM((rows, LANES), BF16),
            pltpu.VMEM((nk, rows, tk), F32),
            pltpu.VMEM((rows, LANES), F32),
            pltpu.VMEM((rows, LANES), F32),
            pltpu.VMEM((rows, LANES), F32),
        ],
        compiler_params=_params("parallel", "parallel"),
        name="gqa_attention",
    )(q, k, v)


def _diff_kernel(q_ref, k_ref, v_ref, lq1_ref, lk1_ref, lq2_ref, lk2_ref, g_ref, o_ref,
                 qm_ref, s_ref, m_ref, l_ref, acc_ref, *, tq, nk, tk, lam_init):
    _stack_masked_heads(q_ref, qm_ref, 1, tq)
    o = _softmax_pv(qm_ref, k_ref, v_ref, s_ref, m_ref, l_ref, acc_ref, nk, tk)
    lam = (jnp.exp(jnp.sum(lq1_ref[...] * lk1_ref[...], axis=-1, keepdims=True))
           - jnp.exp(jnp.sum(lq2_ref[...] * lk2_ref[...], axis=-1, keepdims=True)) + lam_init)
    d = o[0:tq] - lam * o[tq:2 * tq]
    o_ref[...] = (_rmsnorm_rows(d, g_ref[...], SUBLN_EPS) * (1.0 - lam_init)).astype(BF16)


def _diff_attention(q, k, v, lq1, lk1, lq2, lk2, g, batch, seq, lam_init):
    tq = min(ROWS_ATTN // 2, seq)
    tk = min(TK_ATTN, seq)
    nk = seq // tk
    rows = 2 * tq
    nq = seq // tq
    vec = pl.BlockSpec((1, HEAD_DIM), lambda b, h, i: (0, 0))
    return pl.pallas_call(
        functools.partial(_diff_kernel, tq=tq, nk=nk, tk=tk, lam_init=lam_init),
        grid=(batch, DIFF_HEADS, nq),
        in_specs=[
            pl.BlockSpec((tq, LANES), lambda b, h, i: (b * nq + i, h)),
            pl.BlockSpec((seq, LANES), lambda b, h, i: (b, h)),
            pl.BlockSpec((seq, LANES), lambda b, h, i: (b, h)),
            vec, vec, vec, vec,
            pl.BlockSpec((1, LANES), lambda b, h, i: (0, 0)),
        ],
        out_specs=pl.BlockSpec((tq, LANES), lambda b, h, i: (b * nq + i, h)),
        out_shape=jax.ShapeDtypeStruct(q.shape, BF16),
        scratch_shapes=[
            pltpu.VMEM((rows, LANES), BF16),
            pltpu.VMEM((nk, rows, tk), F32),
            pltpu.VMEM((rows, LANES), F32),
            pltpu.VMEM((rows, LANES), F32),
            pltpu.VMEM((rows, LANES), F32),
        ],
        compiler_params=_params("parallel", "parallel", "parallel"),
        name="diff_attention",
    )(q, k, v, lq1, lk1, lq2, lk2, g)


def _na_band_start(n):
    return int(np.clip(n * NA_QCOLS - NA_WIN_COLS // 2, 0, GRID_W - NA_KCOLS))


def _na_bias_indices():
    qr = np.arange(NA_QROWS)[:, None]
    kr = np.arange(NA_KROWS)[None, :]
    half = NA_WIN_ROWS // 2
    key_shift = [0, -half, -(NA_KROWS - NA_QROWS)]
    win_start = [np.maximum(qr - half, 0), qr, np.minimum(qr + half, NA_KROWS - NA_WIN_ROWS)]
    drow = np.stack([np.clip(kr + shift - qr + NA_WIN_ROWS - 1, 0, 2 * NA_WIN_ROWS - 2) for shift in key_shift])
    rmask = np.stack([(kr >= w0) & (kr < w0 + NA_WIN_ROWS) for w0 in win_start])
    qc = np.arange(NA_QCOLS)[:, None]
    kc = np.arange(NA_KCOLS)[None, :]
    dcol, cmask = [], []
    for n in range(GRID_W // NA_QCOLS):
        qabs = n * NA_QCOLS + qc
        kabs = _na_band_start(n) + kc
        w0 = np.clip(qabs - NA_WIN_COLS // 2, 0, GRID_W - NA_WIN_COLS)
        dcol.append(np.clip(kabs - qabs + NA_WIN_COLS - 1, 0, 2 * NA_WIN_COLS - 2))
        cmask.append((kabs >= w0) & (kabs < w0 + NA_WIN_COLS))
    return drow, rmask, np.stack(dcol), np.stack(cmask)


def _na_bias_table(rpb):
    drow, rmask, dcol, cmask = _na_bias_indices()
    oh_r = jnp.asarray(np.eye(2 * NA_WIN_ROWS - 1, dtype=np.float32)[drow])
    oh_c = jnp.asarray(np.eye(2 * NA_WIN_COLS - 1, dtype=np.float32)[dcol])
    hp = lax.Precision.HIGHEST
    a = jnp.einsum("hab,vqka->hvqkb", rpb.astype(F32), oh_r, precision=hp)
    t = jnp.einsum("hvqkb,nxyb->vnhqxky", a, oh_c, precision=hp)
    mask = rmask[:, None, None, :, None, :, None] & cmask[None, :, None, None, :, None, :]
    t = jnp.where(jnp.asarray(mask), t * LOG2E, -jnp.inf)
    return t.reshape(t.shape[:3] + (NA_QROWS * NA_QCOLS, NA_KROWS * NA_KCOLS))


def _na_kernel(q_ref, k0, k1, k2, k3, v0, v1, v2, v3, tbl_ref, o_ref):
    kps = (k0, k1, k2, k3)
    vps = (v0, v1, v2, v3)
    lo = _lane_lo_mask()
    nq = NA_QROWS * NA_QCOLS
    for n in range(GRID_W // NA_QCOLS):
        band = _na_band_start(n)
        for c in range(A_WIDTH // LANES):
            lanes = slice(c * LANES, (c + 1) * LANES)
            qs = jnp.concatenate(
                [q_ref[r * GRID_W + n * NA_QCOLS:r * GRID_W + (n + 1) * NA_QCOLS, lanes] for r in range(NA_QROWS)],
                axis=0)
            zero = jnp.zeros_like(qs)
            qm = jnp.concatenate([jnp.where(lo, qs, zero), jnp.where(lo, zero, qs)], axis=0)

            def band_rows(pieces):
                return jnp.concatenate(
                    [p[r * GRID_W + band:r * GRID_W + band + NA_KCOLS, lanes]
                     for p in pieces for r in range(NA_KPIECE)], axis=0).astype(BF16)

            kb = band_rows(kps)
            vb = band_rows(vps)
            s = _dot_nt(qm, kb)
            s = s + jnp.concatenate([tbl_ref[0, n, 2 * c], tbl_ref[0, n, 2 * c + 1]], axis=0)
            m = jnp.max(s, axis=-1, keepdims=True)
            p = jnp.exp2(s - m)
            l = jnp.sum(p, axis=-1, keepdims=True)
            o = _dot(p.astype(BF16), vb) / l
            res = jnp.where(lo, o[0:nq], o[nq:2 * nq]).astype(BF16)
            for r in range(NA_QROWS):
                o_ref[r * GRID_W + n * NA_QCOLS:r * GRID_W + (n + 1) * NA_QCOLS, lanes] = (
                    res[r * NA_QCOLS:(r + 1) * NA_QCOLS])


def _na_attention(q, k, v, tbl, batch, seq):
    grid_rows = seq // GRID_W
    steps = grid_rows // NA_QROWS
    tq = NA_QROWS * GRID_W
    tp = NA_KPIECE * GRID_W
    pieces_per_batch = grid_rows // NA_KPIECE
    npieces = NA_KROWS // NA_KPIECE
    shift = (NA_WIN_ROWS // 2) // NA_KPIECE

    def piece_map(i):
        def index(b, t):
            start = jnp.clip(t * (NA_QROWS // NA_KPIECE) - shift, 0, pieces_per_batch - npieces)
            return (b * pieces_per_batch + start + i, 0)
        return index

    def tbl_map(b, t):
        variant = jnp.where(t == 0, 0, jnp.where(t == steps - 1, 2, 1))
        return (variant, 0, 0, 0, 0)

    piece_specs = [pl.BlockSpec((tp, A_WIDTH), piece_map(i)) for i in range(npieces)]
    return pl.pallas_call(
        _na_kernel,
        grid=(batch, steps),
        in_specs=[pl.BlockSpec((tq, A_WIDTH), lambda b, t: (b * steps + t, 0))] + piece_specs + piece_specs + [
            pl.BlockSpec((1,) + tbl.shape[1:], tbl_map)],
        out_specs=pl.BlockSpec((tq, A_WIDTH), lambda b, t: (b * steps + t, 0)),
        out_shape=jax.ShapeDtypeStruct(q.shape, BF16),
        compiler_params=_params("parallel", "arbitrary"),
        name="na_attention",
    )(q, k, k, k, k, v, v, v, v, tbl)


def _mlp_kernel(*refs, n_attn, final):
    x_ref = refs[0]
    attn_refs = refs[1:1 + n_attn]
    wo_ref, g_ref, wup_ref, wdown_ref, gf_ref, o_ref, h_ref, acc_ref = refs[1 + n_attn:]
    j = pl.program_id(1)

    @pl.when(j == 0)
    def _():
        x = x_ref[...]
        width = wo_ref.shape[0] // n_attn
        for i, a_ref in enumerate(attn_refs):
            x = x + _dot(a_ref[...], wo_ref[i * width:(i + 1) * width, :])
        acc_ref[...] = x
        h_ref[...] = _rmsnorm_rows(x, g_ref[...], NORM_EPS).astype(BF16)

    u = jnp.maximum(_dot(h_ref[...], wup_ref[...]), 0.0)
    acc_ref[...] += _dot((u * u).astype(BF16), wdown_ref[...])

    @pl.when(j == pl.num_programs(1) - 1)
    def _():
        y = acc_ref[...]
        if final:
            y = _rmsnorm_rows(y, gf_ref[...], NORM_EPS)
        o_ref[...] = y


def _outproj_mlp(x, attn, wo, g, wup, wdown, gf, final):
    n = x.shape[0]
    tm = min(TM_MLP, n)
    tf = TF_MLP
    row = lambda i, j: (i, 0)
    const = lambda i, j: (0, 0)
    return pl.pallas_call(
        functools.partial(_mlp_kernel, n_attn=len(attn), final=final),
        grid=(n // tm, D_FF // tf),
        in_specs=[pl.BlockSpec((tm, D_MODEL), row)]
        + [pl.BlockSpec((tm, a.shape[1]), row) for a in attn]
        + [
            pl.BlockSpec(wo.shape, const),
            pl.BlockSpec((1, D_MODEL), const),
            pl.BlockSpec((D_MODEL, tf), lambda i, j: (0, j)),
            pl.BlockSpec((tf, D_MODEL), lambda i, j: (j, 0)),
            pl.BlockSpec((1, D_MODEL), const),
        ],
        out_specs=pl.BlockSpec((tm, D_MODEL), row),
        out_shape=jax.ShapeDtypeStruct(x.shape, F32),
        scratch_shapes=[pltpu.VMEM((tm, D_MODEL), BF16), pltpu.VMEM((tm, D_MODEL), F32)],
        compiler_params=_params("parallel", "arbitrary"),
        name="outproj_mlp",
    )(x, *attn, wo, g, wup, wdown, gf)


def _rope_angles(pos, dim, theta):
    inv_freq = 1.0 / jnp.power(theta, jnp.arange(0, dim, 2, dtype=F32) / dim)
    ang = pos.astype(F32)[:, None] * inv_freq[None, :]
    return jnp.cos(ang), jnp.sin(ang)


def _axial_tables(seq):
    t = jnp.arange(seq)
    half = HEAD_DIM // 2
    cr, sr = _rope_angles(t // GRID_W, half, AXIAL_THETA)
    cc, sc = _rope_angles(t % GRID_W, half, AXIAL_THETA)
    cos = jnp.concatenate([cr, cr, cc, cc], axis=-1)
    sin = jnp.concatenate([sr, sr, sc, sc], axis=-1)
    return jnp.tile(cos, (1, LANES // HEAD_DIM)), jnp.tile(sin, (1, LANES // HEAD_DIM))


def _rope_tables(seq):
    c, s = _rope_angles(jnp.arange(seq), HEAD_DIM, ROPE_THETA)
    cos = jnp.concatenate([c, c], axis=-1)
    sin = jnp.concatenate([s, s], axis=-1)
    return jnp.tile(cos, (1, LANES // HEAD_DIM)), jnp.tile(sin, (1, LANES // HEAD_DIM))


_q = HEAD_DIM // 4
_AXIAL_SRC = np.concatenate([np.arange(_q, 2 * _q), np.arange(0, _q), np.arange(3 * _q, 4 * _q), np.arange(2 * _q, 3 * _q)])
_AXIAL_SIGN = np.concatenate([-np.ones(_q), np.ones(_q), -np.ones(_q), np.ones(_q)]).astype(np.float32)
_h = HEAD_DIM // 2
_ROPE_SRC = np.concatenate([np.arange(_h, 2 * _h), np.arange(0, _h)])
_ROPE_SIGN = np.concatenate([-np.ones(_h), np.ones(_h)]).astype(np.float32)
_GQA_HEAD_ORDER = np.arange(GQA_Q_HEADS).reshape(GQA_KV_HEADS, -1).T.reshape(-1)


def _rotated_columns(w, heads, src, sign):
    w3 = w.reshape(w.shape[0], heads, HEAD_DIM)
    return (w3[:, :, src] * sign).reshape(w.shape)


def _head_columns(w, order):
    w3 = w.reshape(w.shape[0], -1, HEAD_DIM)
    return w3[:, order, :].reshape(w.shape)


def _prepare_even(w_in, q_norm, k_norm, w_out):
    a, bq, bkv = A_WIDTH, B_Q_WIDTH, B_KV_WIDTH
    wqa, wka, wva = w_in[:, 0:a], w_in[:, a:2 * a], w_in[:, 2 * a:3 * a]
    wqb = _head_columns(w_in[:, 3 * a:3 * a + bq], _GQA_HEAD_ORDER)
    wkb = w_in[:, 3 * a + bq:3 * a + bq + bkv]
    wvb = w_in[:, 3 * a + bq + bkv:]
    w = jnp.concatenate([
        wqa, wka, wva,
        wqb, _rotated_columns(wqb, GQA_Q_HEADS, _AXIAL_SRC, _AXIAL_SIGN),
        wkb, _rotated_columns(wkb, GQA_KV_HEADS, _AXIAL_SRC, _AXIAL_SIGN),
        wvb], axis=1).astype(BF16)
    reps = LANES // HEAD_DIM
    gq = jnp.tile(q_norm, reps)[None]
    gqr = jnp.tile(q_norm[_AXIAL_SRC], reps)[None]
    gk = jnp.tile(k_norm, reps)[None]
    gkr = jnp.tile(k_norm[_AXIAL_SRC], reps)[None]
    wo_b = w_out[a:].reshape(GQA_Q_HEADS, HEAD_DIM, -1)[_GQA_HEAD_ORDER].reshape(bq, -1)
    wo = jnp.concatenate([w_out[:a], wo_b], axis=0).astype(BF16)
    return w, gq, gqr, gk, gkr, wo


def _prepare_odd(w_in):
    d = DIFF_WIDTH
    wq, wk, wv = w_in[:, 0:d], w_in[:, d:2 * d], w_in[:, 2 * d:]
    heads = d // HEAD_DIM
    return jnp.concatenate([
        wq, _rotated_columns(wq, heads, _ROPE_SRC, _ROPE_SIGN),
        wk, _rotated_columns(wk, heads, _ROPE_SRC, _ROPE_SIGN),
        wv], axis=1).astype(BF16)


def _group_ones():
    g = np.arange(LANES) // HEAD_DIM
    return jnp.asarray((g[:, None] == g[None, :]).astype(np.float32), dtype=BF16)


def _trunk(x3, p):
    batch, seq, _ = x3.shape
    assert seq % (NA_KROWS * GRID_W) == 0 and seq % TK_ATTN == 0
    x = x3.reshape(batch * seq, D_MODEL)
    depth = len(p["layers"])
    for layer, lp in enumerate(p["layers"]):
        final = layer == depth - 1
        if layer % 2 == 0:
            cos, sin = _axial_tables(seq)
            qa, ka, va, qb, kb, vb = _inproj_even(x, lp["ln_mix"], lp["w_in"], p["gmat"], lp["gq"], lp["gqr"],
                                                  lp["gk"], lp["gkr"], cos, sin, seq)
            a_out = _na_attention(qa, ka, va, lp["na_bias"], batch, seq)
            b_out = _gqa_attention(qb, kb, vb, batch, seq)
            attn = [a_out, b_out]
        else:
            cos, sin = _rope_tables(seq)
            q, k, v = _inproj_odd(x, lp["ln_mix"], lp["w_in"], cos, sin, seq)
            attn = [_diff_attention(q, k, v, lp["lq1"], lp["lk1"], lp["lq2"], lp["lk2"], lp["subln"],
                                    batch, seq, lp["lam_init"])]
        x = _outproj_mlp(x, attn, lp["w_out"], lp["ln_mlp"], lp["w_up"], lp["w_down"], p["ln_f"], final)
    return x.reshape(batch, seq, D_MODEL)


def kernel(x_prompt, x_sample, ln_mix_e, w_in_e, rpb, q_norm_b, k_norm_b, w_out_e, ln_mix_o, w_in_o, lambda_q1, lambda_k1, lambda_q2, lambda_k2, subln_g, w_out_o, ln_mlp, w_up, w_down, ln_f):
    depth = ln_mlp.shape[0]
    layers = []
    for layer in range(depth):
        j = layer // 2
        lp = {
            "ln_mlp": ln_mlp[layer][None],
            "w_up": w_up[layer].astype(BF16),
            "w_down": w_down[layer].astype(BF16),
        }
        if layer % 2 == 0:
            w, gq, gqr, gk, gkr, wo = _prepare_even(w_in_e[j], q_norm_b[j], k_norm_b[j], w_out_e[j])
            lp.update(ln_mix=ln_mix_e[j][None], w_in=w, gq=gq, gqr=gqr, gk=gk, gkr=gkr, w_out=wo,
                      na_bias=_na_bias_table(rpb[j]))
        else:
            lp.update(ln_mix=ln_mix_o[j][None], w_in=_prepare_odd(w_in_o[j]), w_out=w_out_o[j].astype(BF16),
                      lq1=lambda_q1[j][None], lk1=lambda_k1[j][None], lq2=lambda_q2[j][None],
                      lk2=lambda_k2[j][None], subln=subln_g[j][None],
                      lam_init=0.8 - 0.6 * math.exp(-0.3 * layer))
        layers.append(lp)
    p = {"layers": layers, "gmat": _group_ones(), "ln_f": ln_f[None]}
    return (_trunk(x_prompt, p), _trunk(x_sample, p))
```

```python
import functools
import math

import numpy as np
import jax
import jax.numpy as jnp
from jax import lax
from jax.experimental import pallas as pl
from jax.experimental.pallas import tpu as pltpu

F32 = jnp.float32
BF16 = jnp.bfloat16

D_MODEL = 1024
HEAD_DIM = 64
GRID_W = 64
NA_HEADS = 8
NA_WIN_ROWS = 8
NA_WIN_COLS = 16
NA_QCOLS = 16
NA_KCOLS = 32
GQA_Q_HEADS = 8
GQA_KV_HEADS = 2
AXIAL_THETA = 10000.0
DIFF_HEADS = 8
D_FF = 4 * D_MODEL
ROPE_THETA = 10000.0
NORM_EPS = 1e-6
QK_NORM_EPS = 1e-6
SUBLN_EPS = 1e-5
A_WIDTH = NA_HEADS * HEAD_DIM
B_Q_WIDTH = GQA_Q_HEADS * HEAD_DIM
B_KV_WIDTH = GQA_KV_HEADS * HEAD_DIM
DIFF_WIDTH = 2 * DIFF_HEADS * HEAD_DIM
SM_SCALE = HEAD_DIM ** -0.5
LOG2E = math.log2(math.e)
Q_SCALE = SM_SCALE * LOG2E

LANES = 128
VMEM_LIMIT = 56 * 1024 * 1024

TM_PROJ = 512
TM_MLP = 1024
TF_MLP = 512
TK_ATTN = 2048
ROWS_ATTN = 1024
NA_QROWS = 8
NA_KROWS = 16
NA_KPIECE = 4


def _params(*sem):
    return pltpu.CompilerParams(dimension_semantics=sem, vmem_limit_bytes=VMEM_LIMIT)


def _rmsnorm_rows(x, g, eps):
    ms = jnp.mean(x * x, axis=-1, keepdims=True)
    return x * lax.rsqrt(ms + eps) * g


def _dot(a, b):
    return jnp.dot(a, b, preferred_element_type=F32)


def _dot_nt(a, b):
    return lax.dot_general(a, b, (((1,), (1,)), ((), ())), preferred_element_type=F32)


def _lane_lo_mask():
    return lax.broadcasted_iota(jnp.int32, (1, LANES), 1) < HEAD_DIM


def _group_sumsq(x, gmat_ref):
    sq = x * x
    hi = sq.astype(BF16)
    lo = (sq - hi.astype(F32)).astype(BF16)
    return _dot(hi, gmat_ref[...]) + _dot(lo, gmat_ref[...])


def _inproj_even_kernel(x_ref, g_ref, w_ref, gmat_ref, gq_ref, gqr_ref, gk_ref, gkr_ref, cos_ref, sin_ref,
                        qa_ref, ka_ref, va_ref, qb_ref, kb_ref, vb_ref):
    h = _rmsnorm_rows(x_ref[...], g_ref[...], NORM_EPS).astype(BF16)
    a, bq, bkv = A_WIDTH, B_Q_WIDTH, B_KV_WIDTH
    qa_ref[...] = (_dot(h, w_ref[:, 0:a]) * Q_SCALE).astype(BF16)
    ka_ref[...] = _dot(h, w_ref[:, a:2 * a])
    va_ref[...] = _dot(h, w_ref[:, 2 * a:3 * a])
    o = 3 * a
    q = _dot(h, w_ref[:, o:o + bq])
    qr = _dot(h, w_ref[:, o + bq:o + 2 * bq])
    cos = cos_ref[...]
    sin = sin_ref[...]

    def norm_rope(x, xr, g, gr):
        r = lax.rsqrt(_group_sumsq(x, gmat_ref) * (1.0 / HEAD_DIM) + QK_NORM_EPS)
        return r * ((x * g) * cos + (xr * gr) * sin)

    for c in range(bq // LANES):
        sl = slice(c * LANES, (c + 1) * LANES)
        out = norm_rope(q[:, sl], qr[:, sl], gq_ref[...], gqr_ref[...])
        qb_ref[:, sl] = (out * Q_SCALE).astype(BF16)
    o += 2 * bq
    k = _dot(h, w_ref[:, o:o + bkv])
    kr = _dot(h, w_ref[:, o + bkv:o + 2 * bkv])
    kb_ref[...] = norm_rope(k, kr, gk_ref[...], gkr_ref[...]).astype(BF16)
    o += 2 * bkv
    vb_ref[...] = _dot(h, w_ref[:, o:o + bkv]).astype(BF16)


def _inproj_even(x, g, w, gmat, gq, gqr, gk, gkr, cos, sin, seq):
    n = x.shape[0]
    tm = min(TM_PROJ, seq)
    pos_blocks = seq // tm
    row = lambda i: (i, 0)
    const = lambda i: (0, 0)
    pos = lambda i: (i % pos_blocks, 0)
    wcols = w.shape[1]
    return pl.pallas_call(
        _inproj_even_kernel,
        grid=(n // tm,),
        in_specs=[
            pl.BlockSpec((tm, D_MODEL), row),
            pl.BlockSpec((1, D_MODEL), const),
            pl.BlockSpec((D_MODEL, wcols), const),
            pl.BlockSpec((LANES, LANES), const),
            pl.BlockSpec((1, LANES), const),
            pl.BlockSpec((1, LANES), const),
            pl.BlockSpec((1, LANES), const),
            pl.BlockSpec((1, LANES), const),
            pl.BlockSpec((tm, LANES), pos),
            pl.BlockSpec((tm, LANES), pos),
        ],
        out_specs=[
            pl.BlockSpec((tm, A_WIDTH), row),
            pl.BlockSpec((tm, A_WIDTH), row),
            pl.BlockSpec((tm, A_WIDTH), row),
            pl.BlockSpec((tm, B_Q_WIDTH), row),
            pl.BlockSpec((tm, B_KV_WIDTH), row),
            pl.BlockSpec((tm, B_KV_WIDTH), row),
        ],
        out_shape=[
            jax.ShapeDtypeStruct((n, A_WIDTH), BF16),
            jax.ShapeDtypeStruct((n, A_WIDTH), F32),
            jax.ShapeDtypeStruct((n, A_WIDTH), F32),
            jax.ShapeDtypeStruct((n, B_Q_WIDTH), BF16),
            jax.ShapeDtypeStruct((n, B_KV_WIDTH), BF16),
            jax.ShapeDtypeStruct((n, B_KV_WIDTH), BF16),
        ],
        compiler_params=_params("parallel"),
        name="inproj_even",
    )(x, g, w, gmat, gq, gqr, gk, gkr, cos, sin)


def _inproj_odd_kernel(x_ref, g_ref, w_ref, cos_ref, sin_ref, q_ref, k_ref, v_ref):
    h = _rmsnorm_rows(x_ref[...], g_ref[...], NORM_EPS).astype(BF16)
    d = DIFF_WIDTH
    cos = cos_ref[...]
    sin = sin_ref[...]

    def rope(o, scale, out_ref):
        x = _dot(h, w_ref[:, o:o + d])
        xr = _dot(h, w_ref[:, o + d:o + 2 * d])
        for c in range(d // LANES):
            sl = slice(c * LANES, (c + 1) * LANES)
            out_ref[:, sl] = ((x[:, sl] * cos + xr[:, sl] * sin) * scale).astype(BF16)

    rope(0, Q_SCALE, q_ref)
    rope(2 * d, 1.0, k_ref)
    v_ref[...] = _dot(h, w_ref[:, 4 * d:5 * d]).astype(BF16)


def _inproj_odd(x, g, w, cos, sin, seq):
    n = x.shape[0]
    tm = min(TM_PROJ, seq)
    pos_blocks = seq // tm
    row = lambda i: (i, 0)
    const = lambda i: (0, 0)
    pos = lambda i: (i % pos_blocks, 0)
    out = jax.ShapeDtypeStruct((n, DIFF_WIDTH), BF16)
    return pl.pallas_call(
        _inproj_odd_kernel,
        grid=(n // tm,),
        in_specs=[
            pl.BlockSpec((tm, D_MODEL), row),
            pl.BlockSpec((1, D_MODEL), const),
            pl.BlockSpec((D_MODEL, w.shape[1]), const),
            pl.BlockSpec((tm, LANES), pos),
            pl.BlockSpec((tm, LANES), pos),
        ],
        out_specs=[pl.BlockSpec((tm, DIFF_WIDTH), row)] * 3,
        out_shape=[out, out, out],
        compiler_params=_params("parallel"),
        name="inproj_odd",
    )(x, g, w, cos, sin)


def _stack_masked_heads(q_ref, qm_ref, ncols, tq):
    lo = _lane_lo_mask()
    for c in range(ncols):
        qc = q_ref[:, c * LANES:(c + 1) * LANES]
        zero = jnp.zeros_like(qc)
        qm_ref[2 * c * tq:(2 * c + 1) * tq, :] = jnp.where(lo, qc, zero)
        qm_ref[(2 * c + 1) * tq:(2 * c + 2) * tq, :] = jnp.where(lo, zero, qc)


def _softmax_pv(first_tile, qm_ref, k_ref, v_ref, s_ref, m_ref, vext_ref, acc_ref, nk, tk):
    @pl.when(first_tile)
    def _():
        vext_ref[:, 0:LANES] = v_ref[...]
        vext_ref[:, LANES:2 * LANES] = jnp.ones(v_ref.shape, BF16)

    m_ref[...] = jnp.full(m_ref.shape, -jnp.inf, F32)

    def scores(j, carry):
        kj = k_ref[pl.ds(pl.multiple_of(j * tk, tk), tk), :]
        s = _dot_nt(qm_ref[...], kj)
        s_ref[j] = s
        m = m_ref[...]
        for i in range(tk // LANES):
            m = jnp.maximum(m, s[:, i * LANES:(i + 1) * LANES])
        m_ref[...] = m
        return carry

    lax.fori_loop(0, nk, scores, 0)
    m_ref[...] = jnp.broadcast_to(jnp.max(m_ref[...], axis=-1, keepdims=True), m_ref.shape)
    acc_ref[...] = jnp.zeros(acc_ref.shape, F32)

    def values(j, carry):
        s = s_ref[j]
        m = m_ref[...]
        ps = [jnp.exp2(s[:, i * LANES:(i + 1) * LANES] - m).astype(BF16) for i in range(tk // LANES)]
        vj = vext_ref[pl.ds(pl.multiple_of(j * tk, tk), tk), :]
        acc_ref[...] += _dot(jnp.concatenate(ps, axis=1), vj)
        return carry

    lax.fori_loop(0, nk, values, 0)
    return acc_ref[:, 0:LANES] / acc_ref[:, LANES:2 * LANES]


def _attn_scratch(rows, nk, tk, seq):
    return [
        pltpu.VMEM((rows, LANES), BF16),
        pltpu.VMEM((nk, rows, tk), F32),
        pltpu.VMEM((rows, LANES), F32),
        pltpu.VMEM((seq, 2 * LANES), BF16),
        pltpu.VMEM((rows, 2 * LANES), F32),
    ]


def _gqa_kernel(q_ref, k_ref, v_ref, o_ref, qm_ref, s_ref, m_ref, vext_ref, acc_ref, *, tq, nk, tk):
    ncols = B_Q_WIDTH // LANES
    _stack_masked_heads(q_ref, qm_ref, ncols, tq)
    o = _softmax_pv(pl.program_id(1) == 0, qm_ref, k_ref, v_ref, s_ref, m_ref, vext_ref, acc_ref, nk, tk)
    lo = _lane_lo_mask()
    for c in range(ncols):
        o_lo = o[2 * c * tq:(2 * c + 1) * tq]
        o_hi = o[(2 * c + 1) * tq:(2 * c + 2) * tq]
        o_ref[:, c * LANES:(c + 1) * LANES] = jnp.where(lo, o_lo, o_hi).astype(BF16)


def _gqa_attention(q, k, v, batch, seq):
    ncols = B_Q_WIDTH // LANES
    tq = min(ROWS_ATTN // (2 * ncols), seq)
    tk = min(TK_ATTN, seq)
    nk = seq // tk
    rows = 2 * ncols * tq
    nq = seq // tq
    return pl.pallas_call(
        functools.partial(_gqa_kernel, tq=tq, nk=nk, tk=tk),
        grid=(batch, nq),
        in_specs=[
            pl.BlockSpec((tq, B_Q_WIDTH), lambda b, i: (b * nq + i, 0)),
            pl.BlockSpec((seq, B_KV_WIDTH), lambda b, i: (b, 0)),
            pl.BlockSpec((seq, B_KV_WIDTH), lambda b, i: (b, 0)),
        ],
        out_specs=pl.BlockSpec((tq, B_Q_WIDTH), lambda b, i: (b * nq + i, 0)),
        out_shape=jax.ShapeDtypeStruct(q.shape, BF16),
        scratch_shapes=_attn_scratch(rows, nk, tk, seq),
        compiler_params=_params("parallel", "arbitrary"),
        name="gqa_attention",
    )(q, k, v)


def _diff_kernel(q_ref, k_ref, v_ref, lq1_ref, lk1_ref, lq2_ref, lk2_ref, g_ref, o_ref,
                 qm_ref, s_ref, m_ref, vext_ref, acc_ref, *, tq, nk, tk, lam_init):
    _stack_masked_heads(q_ref, qm_ref, 1, tq)
    o = _softmax_pv(pl.program_id(2) == 0, qm_ref, k_ref, v_ref, s_ref, m_ref, vext_ref, acc_ref, nk, tk)
    lam = (jnp.exp(jnp.sum(lq1_ref[...] * lk1_ref[...], axis=-1, keepdims=True))
           - jnp.exp(jnp.sum(lq2_ref[...] * lk2_ref[...], axis=-1, keepdims=True)) + lam_init)
    d = o[0:tq] - lam * o[tq:2 * tq]
    o_ref[...] = (_rmsnorm_rows(d, g_ref[...], SUBLN_EPS) * (1.0 - lam_init)).astype(BF16)


def _diff_attention(q, k, v, lq1, lk1, lq2, lk2, g, batch, seq, lam_init):
    tq = min(ROWS_ATTN // 2, seq)
    tk = min(TK_ATTN, seq)
    nk = seq // tk
    rows = 2 * tq
    nq = seq // tq
    vec = pl.BlockSpec((1, HEAD_DIM), lambda b, h, i: (0, 0))
    return pl.pallas_call(
        functools.partial(_diff_kernel, tq=tq, nk=nk, tk=tk, lam_init=lam_init),
        grid=(batch, DIFF_HEADS, nq),
        in_specs=[
            pl.BlockSpec((tq, LANES), lambda b, h, i: (b * nq + i, h)),
            pl.BlockSpec((seq, LANES), lambda b, h, i: (b, h)),
            pl.BlockSpec((seq, LANES), lambda b, h, i: (b, h)),
            vec, vec, vec, vec,
            pl.BlockSpec((1, LANES), lambda b, h, i: (0, 0)),
        ],
        out_specs=pl.BlockSpec((tq, LANES), lambda b, h, i: (b * nq + i, h)),
        out_shape=jax.ShapeDtypeStruct(q.shape, BF16),
        scratch_shapes=_attn_scratch(rows, nk, tk, seq),
        compiler_params=_params("parallel", "parallel", "arbitrary"),
        name="diff_attention",
    )(q, k, v, lq1, lk1, lq2, lk2, g)


def _na_band_start(n):
    return int(np.clip(n * NA_QCOLS - NA_WIN_COLS // 2, 0, GRID_W - NA_KCOLS))


def _na_bias_indices():
    qr = np.arange(NA_QROWS)[:, None]
    kr = np.arange(NA_KROWS)[None, :]
    half = NA_WIN_ROWS // 2
    key_shift = [0, -half, -(NA_KROWS - NA_QROWS)]
    win_start = [np.maximum(qr - half, 0), qr, np.minimum(qr + half, NA_KROWS - NA_WIN_ROWS)]
    drow = np.stack([np.clip(kr + shift - qr + NA_WIN_ROWS - 1, 0, 2 * NA_WIN_ROWS - 2) for shift in key_shift])
    rmask = np.stack([(kr >= w0) & (kr < w0 + NA_WIN_ROWS) for w0 in win_start])
    qc = np.arange(NA_QCOLS)[:, None]
    kc = np.arange(NA_KCOLS)[None, :]
    dcol, cmask = [], []
    for n in range(GRID_W // NA_QCOLS):
        qabs = n * NA_QCOLS + qc
        kabs = _na_band_start(n) + kc
        w0 = np.clip(qabs - NA_WIN_COLS // 2, 0, GRID_W - NA_WIN_COLS)
        dcol.append(np.clip(kabs - qabs + NA_WIN_COLS - 1, 0, 2 * NA_WIN_COLS - 2))
        cmask.append((kabs >= w0) & (kabs < w0 + NA_WIN_COLS))
    return drow, rmask, np.stack(dcol), np.stack(cmask)


def _na_bias_table(rpb):
    drow, rmask, dcol, cmask = _na_bias_indices()
    oh_r = jnp.asarray(np.eye(2 * NA_WIN_ROWS - 1, dtype=np.float32)[drow])
    oh_c = jnp.asarray(np.eye(2 * NA_WIN_COLS - 1, dtype=np.float32)[dcol])
    hp = lax.Precision.HIGHEST
    a = jnp.einsum("hab,vqka->hvqkb", rpb.astype(F32), oh_r, precision=hp)
    t = jnp.einsum("hvqkb,nxyb->vnhqxky", a, oh_c, precision=hp)
    mask = rmask[:, None, None, :, None, :, None] & cmask[None, :, None, None, :, None, :]
    t = jnp.where(jnp.asarray(mask), t * LOG2E, -jnp.inf)
    return t.reshape(t.shape[:3] + (NA_QROWS * NA_QCOLS, NA_KROWS * NA_KCOLS))


def _na_kernel(q_ref, k0, k1, k2, k3, v0, v1, v2, v3, tbl_ref, o_ref):
    kps = (k0, k1, k2, k3)
    vps = (v0, v1, v2, v3)
    lo = _lane_lo_mask()
    nq = NA_QROWS * NA_QCOLS
    for n in range(GRID_W // NA_QCOLS):
        band = _na_band_start(n)
        for c in range(A_WIDTH // LANES):
            lanes = slice(c * LANES, (c + 1) * LANES)
            qs = jnp.concatenate(
                [q_ref[r * GRID_W + n * NA_QCOLS:r * GRID_W + (n + 1) * NA_QCOLS, lanes] for r in range(NA_QROWS)],
                axis=0)
            zero = jnp.zeros_like(qs)
            qm = jnp.concatenate([jnp.where(lo, qs, zero), jnp.where(lo, zero, qs)], axis=0)

            def band_rows(pieces):
                return jnp.concatenate(
                    [p[r * GRID_W + band:r * GRID_W + band + NA_KCOLS, lanes]
                     for p in pieces for r in range(NA_KPIECE)], axis=0).astype(BF16)

            kb = band_rows(kps)
            vb = band_rows(vps)
            s = _dot_nt(qm, kb)
            s = s + jnp.concatenate([tbl_ref[0, n, 2 * c], tbl_ref[0, n, 2 * c + 1]], axis=0)
            m = jnp.max(s, axis=-1, keepdims=True)
            p = jnp.exp2(s - m)
            l = jnp.sum(p, axis=-1, keepdims=True)
            o = _dot(p.astype(BF16), vb) / l
            res = jnp.where(lo, o[0:nq], o[nq:2 * nq]).astype(BF16)
            for r in range(NA_QROWS):
                o_ref[r * GRID_W + n * NA_QCOLS:r * GRID_W + (n + 1) * NA_QCOLS, lanes] = (
                    res[r * NA_QCOLS:(r + 1) * NA_QCOLS])


def _na_attention(q, k, v, tbl, batch, seq):
    grid_rows = seq // GRID_W
    steps = grid_rows // NA_QROWS
    tq = NA_QROWS * GRID_W
    tp = NA_KPIECE * GRID_W
    pieces_per_batch = grid_rows // NA_KPIECE
    npieces = NA_KROWS // NA_KPIECE
    shift = (NA_WIN_ROWS // 2) // NA_KPIECE

    def piece_map(i):
        def index(b, t):
            start = jnp.clip(t * (NA_QROWS // NA_KPIECE) - shift, 0, pieces_per_batch - npieces)
            return (b * pieces_per_batch + start + i, 0)
        return index

    def tbl_map(b, t):
        variant = jnp.where(t == 0, 0, jnp.where(t == steps - 1, 2, 1))
        return (variant, 0, 0, 0, 0)

    piece_specs = [pl.BlockSpec((tp, A_WIDTH), piece_map(i)) for i in range(npieces)]
    return pl.pallas_call(
        _na_kernel,
        grid=(batch, steps),
        in_specs=[pl.BlockSpec((tq, A_WIDTH), lambda b, t: (b * steps + t, 0))] + piece_specs + piece_specs + [
            pl.BlockSpec((1,) + tbl.shape[1:], tbl_map)],
        out_specs=pl.BlockSpec((tq, A_WIDTH), lambda b, t: (b * steps + t, 0)),
        out_shape=jax.ShapeDtypeStruct(q.shape, BF16),
        compiler_params=_params("parallel", "arbitrary"),
        name="na_attention",
    )(q, k, k, k, k, v, v, v, v, tbl)


def _mlp_kernel(*refs, n_attn, final):
    x_ref = refs[0]
    attn_refs = refs[1:1 + n_attn]
    wo_ref, g_ref, wup_ref, wdown_ref, gf_ref, o_ref, h_ref, acc_ref = refs[1 + n_attn:]
    j = pl.program_id(1)

    @pl.when(j == 0)
    def _():
        x = x_ref[...]
        width = wo_ref.shape[0] // n_attn
        for i, a_ref in enumerate(attn_refs):
            x = x + _dot(a_ref[...], wo_ref[i * width:(i + 1) * width, :])
        acc_ref[...] = x
        h_ref[...] = _rmsnorm_rows(x, g_ref[...], NORM_EPS).astype(BF16)

    u = jnp.maximum(_dot(h_ref[...], wup_ref[...]), 0.0)
    acc_ref[...] += _dot((u * u).astype(BF16), wdown_ref[...])

    @pl.when(j == pl.num_programs(1) - 1)
    def _():
        y = acc_ref[...]
        if final:
            y = _rmsnorm_rows(y, gf_ref[...], NORM_EPS)
        o_ref[...] = y


def _outproj_mlp(x, attn, wo, g, wup, wdown, gf, final):
    n = x.shape[0]
    tm = min(TM_MLP, n)
    tf = TF_MLP
    row = lambda i, j: (i, 0)
    const = lambda i, j: (0, 0)
    return pl.pallas_call(
        functools.partial(_mlp_kernel, n_attn=len(attn), final=final),
        grid=(n // tm, D_FF // tf),
        in_specs=[pl.BlockSpec((tm, D_MODEL), row)]
        + [pl.BlockSpec((tm, a.shape[1]), row) for a in attn]
        + [
            pl.BlockSpec(wo.shape, const),
            pl.BlockSpec((1, D_MODEL), const),
            pl.BlockSpec((D_MODEL, tf), lambda i, j: (0, j)),
            pl.BlockSpec((tf, D_MODEL), lambda i, j: (j, 0)),
            pl.BlockSpec((1, D_MODEL), const),
        ],
        out_specs=pl.BlockSpec((tm, D_MODEL), row),
        out_shape=jax.ShapeDtypeStruct(x.shape, F32),
        scratch_shapes=[pltpu.VMEM((tm, D_MODEL), BF16), pltpu.VMEM((tm, D_MODEL), F32)],
        compiler_params=_params("parallel", "arbitrary"),
        name="outproj_mlp",
    )(x, *attn, wo, g, wup, wdown, gf)


def _rope_angles(pos, dim, theta):
    inv_freq = 1.0 / jnp.power(theta, jnp.arange(0, dim, 2, dtype=F32) / dim)
    ang = pos.astype(F32)[:, None] * inv_freq[None, :]
    return jnp.cos(ang), jnp.sin(ang)


def _axial_tables(seq):
    t = jnp.arange(seq)
    half = HEAD_DIM // 2
    cr, sr = _rope_angles(t // GRID_W, half, AXIAL_THETA)
    cc, sc = _rope_angles(t % GRID_W, half, AXIAL_THETA)
    cos = jnp.concatenate([cr, cr, cc, cc], axis=-1)
    sin = jnp.concatenate([sr, sr, sc, sc], axis=-1)
    return jnp.tile(cos, (1, LANES // HEAD_DIM)), jnp.tile(sin, (1, LANES // HEAD_DIM))


def _rope_tables(seq):
    c, s = _rope_angles(jnp.arange(seq), HEAD_DIM, ROPE_THETA)
    cos = jnp.concatenate([c, c], axis=-1)
    sin = jnp.concatenate([s, s], axis=-1)
    return jnp.tile(cos, (1, LANES // HEAD_DIM)), jnp.tile(sin, (1, LANES // HEAD_DIM))


_q = HEAD_DIM // 4
_AXIAL_SRC = np.concatenate([np.arange(_q, 2 * _q), np.arange(0, _q), np.arange(3 * _q, 4 * _q), np.arange(2 * _q, 3 * _q)])
_AXIAL_SIGN = np.concatenate([-np.ones(_q), np.ones(_q), -np.ones(_q), np.ones(_q)]).astype(np.float32)
_h = HEAD_DIM // 2
_ROPE_SRC = np.concatenate([np.arange(_h, 2 * _h), np.arange(0, _h)])
_ROPE_SIGN = np.concatenate([-np.ones(_h), np.ones(_h)]).astype(np.float32)
_GQA_HEAD_ORDER = np.arange(GQA_Q_HEADS).reshape(GQA_KV_HEADS, -1).T.reshape(-1)


def _rotated_columns(w, heads, src, sign):
    w3 = w.reshape(w.shape[0], heads, HEAD_DIM)
    return (w3[:, :, src] * sign).reshape(w.shape)


def _head_columns(w, order):
    w3 = w.reshape(w.shape[0], -1, HEAD_DIM)
    return w3[:, order, :].reshape(w.shape)


def _prepare_even(w_in, q_norm, k_norm, w_out):
    a, bq, bkv = A_WIDTH, B_Q_WIDTH, B_KV_WIDTH
    wqa, wka, wva = w_in[:, 0:a], w_in[:, a:2 * a], w_in[:, 2 * a:3 * a]
    wqb = _head_columns(w_in[:, 3 * a:3 * a + bq], _GQA_HEAD_ORDER)
    wkb = w_in[:, 3 * a + bq:3 * a + bq + bkv]
    wvb = w_in[:, 3 * a + bq + bkv:]
    w = jnp.concatenate([
        wqa, wka, wva,
        wqb, _rotated_columns(wqb, GQA_Q_HEADS, _AXIAL_SRC, _AXIAL_SIGN),
        wkb, _rotated_columns(wkb, GQA_KV_HEADS, _AXIAL_SRC, _AXIAL_SIGN),
        wvb], axis=1).astype(BF16)
    reps = LANES // HEAD_DIM
    gq = jnp.tile(q_norm, reps)[None]
    gqr = jnp.tile(q_norm[_AXIAL_SRC], reps)[None]
    gk = jnp.tile(k_norm, reps)[None]
    gkr = jnp.tile(k_norm[_AXIAL_SRC], reps)[None]
    wo_b = w_out[a:].reshape(GQA_Q_HEADS, HEAD_DIM, -1)[_GQA_HEAD_ORDER].reshape(bq, -1)
    wo = jnp.concatenate([w_out[:a], wo_b], axis=0).astype(BF16)
    return w, gq, gqr, gk, gkr, wo


def _prepare_odd(w_in):
    d = DIFF_WIDTH
    wq, wk, wv = w_in[:, 0:d], w_in[:, d:2 * d], w_in[:, 2 * d:]
    heads = d // HEAD_DIM
    return jnp.concatenate([
        wq, _rotated_columns(wq, heads, _ROPE_SRC, _ROPE_SIGN),
        wk, _rotated_columns(wk, heads, _ROPE_SRC, _ROPE_SIGN),
        wv], axis=1).astype(BF16)


def _group_ones():
    g = np.arange(LANES) // HEAD_DIM
    return jnp.asarray((g[:, None] == g[None, :]).astype(np.float32), dtype=BF16)


def _trunk(x3, p):
    batch, seq, _ = x3.shape
    assert seq % (NA_KROWS * GRID_W) == 0 and seq % TK_ATTN == 0
    x = x3.reshape(batch * seq, D_MODEL)
    depth = len(p["layers"])
    for layer, lp in enumerate(p["layers"]):
        final = layer == depth - 1
        if layer % 2 == 0:
            cos, sin = _axial_tables(seq)
            qa, ka, va, qb, kb, vb = _inproj_even(x, lp["ln_mix"], lp["w_in"], p["gmat"], lp["gq"], lp["gqr"],
                                                  lp["gk"], lp["gkr"], cos, sin, seq)
            a_out = _na_attention(qa, ka, va, lp["na_bias"], batch, seq)
            b_out = _gqa_attention(qb, kb, vb, batch, seq)
            attn = [a_out, b_out]
        else:
            cos, sin = _rope_tables(seq)
            q, k, v = _inproj_odd(x, lp["ln_mix"], lp["w_in"], cos, sin, seq)
            attn = [_diff_attention(q, k, v, lp["lq1"], lp["lk1"], lp["lq2"], lp["lk2"], lp["subln"],
                                    batch, seq, lp["lam_init"])]
        x = _outproj_mlp(x, attn, lp["w_out"], lp["ln_mlp"], lp["w_up"], lp["w_down"], p["ln_f"], final)
    return x.reshape(batch, seq, D_MODEL)


def kernel(x_prompt, x_sample, ln_mix_e, w_in_e, rpb, q_norm_b, k_norm_b, w_out_e, ln_mix_o, w_in_o, lambda_q1, lambda_k1, lambda_q2, lambda_k2, subln_g, w_out_o, ln_mlp, w_up, w_down, ln_f):
    depth = ln_mlp.shape[0]
    layers = []
    for layer in range(depth):
        j = layer // 2
        lp = {
            "ln_mlp": ln_mlp[layer][None],
            "w_up": w_up[layer].astype(BF16),
            "w_down": w_down[layer].astype(BF16),
        }
        if layer % 2 == 0:
            w, gq, gqr, gk, gkr, wo = _prepare_even(w_in_e[j], q_norm_b[j], k_norm_b[j], w_out_e[j])
            lp.update(ln_mix=ln_mix_e[j][None], w_in=w, gq=gq, gqr=gqr, gk=gk, gkr=gkr, w_out=wo,
                      na_bias=_na_bias_table(rpb[j]))
        else:
            lp.update(ln_mix=ln_mix_o[j][None], w_in=_prepare_odd(w_in_o[j]), w_out=w_out_o[j].astype(BF16),
                      lq1=lambda_q1[j][None], lk1=lambda_k1[j][None], lq2=lambda_q2[j][None],
                      lk2=lambda_k2[j][None], subln=subln_g[j][None],
                      lam_init=0.8 - 0.6 * math.exp(-0.3 * layer))
        layers.append(lp)
    p = {"layers": layers, "gmat": _group_ones(), "ln_f": ln_f[None]}
    return (_trunk(x_prompt, p), _trunk(x_sample, p))
```

```python
import functools
import math

import numpy as np
import jax
import jax.numpy as jnp
from jax import lax
from jax.experimental import pallas as pl
from jax.experimental.pallas import tpu as pltpu

F32 = jnp.float32
BF16 = jnp.bfloat16

D_MODEL = 1024
HEAD_DIM = 64
GRID_W = 64
NA_HEADS = 8
NA_WIN_ROWS = 8
NA_WIN_COLS = 16
NA_QCOLS = 16
NA_KCOLS = 32
GQA_Q_HEADS = 8
GQA_KV_HEADS = 2
AXIAL_THETA = 10000.0
DIFF_HEADS = 8
D_FF = 4 * D_MODEL
ROPE_THETA = 10000.0
NORM_EPS = 1e-6
QK_NORM_EPS = 1e-6
SUBLN_EPS = 1e-5
A_WIDTH = NA_HEADS * HEAD_DIM
B_Q_WIDTH = GQA_Q_HEADS * HEAD_DIM
B_KV_WIDTH = GQA_KV_HEADS * HEAD_DIM
DIFF_WIDTH = 2 * DIFF_HEADS * HEAD_DIM
SM_SCALE = HEAD_DIM ** -0.5
LOG2E = math.log2(math.e)
Q_SCALE = SM_SCALE * LOG2E

LANES = 128
VMEM_LIMIT = 56 * 1024 * 1024

TM_PROJ = 512
TM_MLP = 1024
TF_MLP = 512
TK_ATTN = 4096
ROWS_ATTN = 2048
NA_QROWS = 8
NA_KROWS = 16
NA_KPIECE = 4


def _params(*sem):
    return pltpu.CompilerParams(dimension_semantics=sem, vmem_limit_bytes=VMEM_LIMIT)


def _rmsnorm_rows(x, g, eps):
    ms = jnp.mean(x * x, axis=-1, keepdims=True)
    return x * lax.rsqrt(ms + eps) * g


def _dot(a, b):
    return jnp.dot(a, b, preferred_element_type=F32)


def _dot_nt(a, b):
    return lax.dot_general(a, b, (((1,), (1,)), ((), ())), preferred_element_type=F32)


def _lane_lo_mask():
    return lax.broadcasted_iota(jnp.int32, (1, LANES), 1) < HEAD_DIM


def _group_sumsq(x, gmat_ref):
    sq = x * x
    hi = sq.astype(BF16)
    lo = (sq - hi.astype(F32)).astype(BF16)
    return _dot(hi, gmat_ref[...]) + _dot(lo, gmat_ref[...])


def _inproj_even_kernel(x_ref, g_ref, w_ref, gmat_ref, gq_ref, gqr_ref, gk_ref, gkr_ref, cos_ref, sin_ref,
                        qa_ref, ka_ref, va_ref, qb_ref, kb_ref, vb_ref):
    h = _rmsnorm_rows(x_ref[...], g_ref[...], NORM_EPS).astype(BF16)
    a, bq, bkv = A_WIDTH, B_Q_WIDTH, B_KV_WIDTH
    qa_ref[...] = (_dot(h, w_ref[:, 0:a]) * Q_SCALE).astype(BF16)
    ka_ref[...] = _dot(h, w_ref[:, a:2 * a])
    va_ref[...] = _dot(h, w_ref[:, 2 * a:3 * a])
    o = 3 * a
    q = _dot(h, w_ref[:, o:o + bq])
    qr = _dot(h, w_ref[:, o + bq:o + 2 * bq])
    cos = cos_ref[...]
    sin = sin_ref[...]

    def norm_rope(x, xr, g, gr):
        r = lax.rsqrt(_group_sumsq(x, gmat_ref) * (1.0 / HEAD_DIM) + QK_NORM_EPS)
        return r * ((x * g) * cos + (xr * gr) * sin)

    for c in range(bq // LANES):
        sl = slice(c * LANES, (c + 1) * LANES)
        out = norm_rope(q[:, sl], qr[:, sl], gq_ref[...], gqr_ref[...])
        qb_ref[:, sl] = (out * Q_SCALE).astype(BF16)
    o += 2 * bq
    k = _dot(h, w_ref[:, o:o + bkv])
    kr = _dot(h, w_ref[:, o + bkv:o + 2 * bkv])
    kb_ref[...] = norm_rope(k, kr, gk_ref[...], gkr_ref[...]).astype(BF16)
    o += 2 * bkv
    vb_ref[...] = _dot(h, w_ref[:, o:o + bkv]).astype(BF16)


def _inproj_even(x, g, w, gmat, gq, gqr, gk, gkr, cos, sin, seq):
    n = x.shape[0]
    tm = min(TM_PROJ, seq)
    pos_blocks = seq // tm
    row = lambda i: (i, 0)
    const = lambda i: (0, 0)
    pos = lambda i: (i % pos_blocks, 0)
    wcols = w.shape[1]
    return pl.pallas_call(
        _inproj_even_kernel,
        grid=(n // tm,),
        in_specs=[
            pl.BlockSpec((tm, D_MODEL), row),
            pl.BlockSpec((1, D_MODEL), const),
            pl.BlockSpec((D_MODEL, wcols), const),
            pl.BlockSpec((LANES, LANES), const),
            pl.BlockSpec((1, LANES), const),
            pl.BlockSpec((1, LANES), const),
            pl.BlockSpec((1, LANES), const),
            pl.BlockSpec((1, LANES), const),
            pl.BlockSpec((tm, LANES), pos),
            pl.BlockSpec((tm, LANES), pos),
        ],
        out_specs=[
            pl.BlockSpec((tm, A_WIDTH), row),
            pl.BlockSpec((tm, A_WIDTH), row),
            pl.BlockSpec((tm, A_WIDTH), row),
            pl.BlockSpec((tm, B_Q_WIDTH), row),
            pl.BlockSpec((tm, B_KV_WIDTH), row),
            pl.BlockSpec((tm, B_KV_WIDTH), row),
        ],
        out_shape=[
            jax.ShapeDtypeStruct((n, A_WIDTH), BF16),
            jax.ShapeDtypeStruct((n, A_WIDTH), F32),
            jax.ShapeDtypeStruct((n, A_WIDTH), F32),
            jax.ShapeDtypeStruct((n, B_Q_WIDTH), BF16),
            jax.ShapeDtypeStruct((n, B_KV_WIDTH), BF16),
            jax.ShapeDtypeStruct((n, B_KV_WIDTH), BF16),
        ],
        compiler_params=_params("parallel"),
        name="inproj_even",
    )(x, g, w, gmat, gq, gqr, gk, gkr, cos, sin)


def _inproj_odd_kernel(x_ref, g_ref, w_ref, cos_ref, sin_ref, q_ref, k_ref, v_ref):
    h = _rmsnorm_rows(x_ref[...], g_ref[...], NORM_EPS).astype(BF16)
    d = DIFF_WIDTH
    cos = cos_ref[...]
    sin = sin_ref[...]

    def rope(o, scale, out_ref):
        x = _dot(h, w_ref[:, o:o + d])
        xr = _dot(h, w_ref[:, o + d:o + 2 * d])
        for c in range(d // LANES):
            sl = slice(c * LANES, (c + 1) * LANES)
            out_ref[:, sl] = ((x[:, sl] * cos + xr[:, sl] * sin) * scale).astype(BF16)

    rope(0, Q_SCALE, q_ref)
    rope(2 * d, 1.0, k_ref)
    v_ref[...] = _dot(h, w_ref[:, 4 * d:5 * d]).astype(BF16)


def _inproj_odd(x, g, w, cos, sin, seq):
    n = x.shape[0]
    tm = min(TM_PROJ, seq)
    pos_blocks = seq // tm
    row = lambda i: (i, 0)
    const = lambda i: (0, 0)
    pos = lambda i: (i % pos_blocks, 0)
    out = jax.ShapeDtypeStruct((n, DIFF_WIDTH), BF16)
    return pl.pallas_call(
        _inproj_odd_kernel,
        grid=(n // tm,),
        in_specs=[
            pl.BlockSpec((tm, D_MODEL), row),
            pl.BlockSpec((1, D_MODEL), const),
            pl.BlockSpec((D_MODEL, w.shape[1]), const),
            pl.BlockSpec((tm, LANES), pos),
            pl.BlockSpec((tm, LANES), pos),
        ],
        out_specs=[pl.BlockSpec((tm, DIFF_WIDTH), row)] * 3,
        out_shape=[out, out, out],
        compiler_params=_params("parallel"),
        name="inproj_odd",
    )(x, g, w, cos, sin)


def _stack_masked_heads(q_ref, qm_ref, ncols, tq):
    lo = _lane_lo_mask()
    for c in range(ncols):
        qc = q_ref[:, c * LANES:(c + 1) * LANES]
        zero = jnp.zeros_like(qc)
        qm_ref[2 * c * tq:(2 * c + 1) * tq, :] = jnp.where(lo, qc, zero)
        qm_ref[(2 * c + 1) * tq:(2 * c + 2) * tq, :] = jnp.where(lo, zero, qc)


def _softmax_pv(first_tile, qm_ref, k_ref, v_ref, s_ref, m_ref, vext_ref, acc_ref, nk, tk):
    @pl.when(first_tile)
    def _():
        vext_ref[:, 0:LANES] = v_ref[...]
        vext_ref[:, LANES:2 * LANES] = jnp.ones(v_ref.shape, BF16)

    m_ref[...] = jnp.full(m_ref.shape, -jnp.inf, F32)

    def scores(j, carry):
        kj = k_ref[pl.ds(pl.multiple_of(j * tk, tk), tk), :]
        s = _dot_nt(qm_ref[...], kj)
        s_ref[j] = s
        m = m_ref[...]
        for i in range(tk // LANES):
            m = jnp.maximum(m, s[:, i * LANES:(i + 1) * LANES])
        m_ref[...] = m
        return carry

    lax.fori_loop(0, nk, scores, 0)
    m_ref[...] = jnp.broadcast_to(jnp.max(m_ref[...], axis=-1, keepdims=True), m_ref.shape)
    acc_ref[...] = jnp.zeros(acc_ref.shape, F32)

    def values(j, carry):
        s = s_ref[j]
        m = m_ref[...]
        ps = [jnp.exp2(s[:, i * LANES:(i + 1) * LANES] - m).astype(BF16) for i in range(tk // LANES)]
        vj = vext_ref[pl.ds(pl.multiple_of(j * tk, tk), tk), :]
        acc_ref[...] += _dot(jnp.concatenate(ps, axis=1), vj)
        return carry

    lax.fori_loop(0, nk, values, 0)
    return acc_ref[:, 0:LANES] / acc_ref[:, LANES:2 * LANES]


_SMEM_SPEC = pl.BlockSpec(memory_space=pltpu.SMEM)


def _trip_count(nk):
    return jnp.full((1,), nk, jnp.int32)


def _attn_scratch(rows, nk, tk, seq):
    return [
        pltpu.VMEM((rows, LANES), BF16),
        pltpu.VMEM((nk, rows, tk), F32),
        pltpu.VMEM((rows, LANES), F32),
        pltpu.VMEM((seq, 2 * LANES), BF16),
        pltpu.VMEM((rows, 2 * LANES), F32),
    ]


def _gqa_kernel(nk_ref, q_ref, k_ref, v_ref, o_ref, qm_ref, s_ref, m_ref, vext_ref, acc_ref, *, tq, tk):
    ncols = B_Q_WIDTH // LANES
    _stack_masked_heads(q_ref, qm_ref, ncols, tq)
    o = _softmax_pv(pl.program_id(1) == 0, qm_ref, k_ref, v_ref, s_ref, m_ref, vext_ref, acc_ref, nk_ref[0], tk)
    lo = _lane_lo_mask()
    for c in range(ncols):
        o_lo = o[2 * c * tq:(2 * c + 1) * tq]
        o_hi = o[(2 * c + 1) * tq:(2 * c + 2) * tq]
        o_ref[:, c * LANES:(c + 1) * LANES] = jnp.where(lo, o_lo, o_hi).astype(BF16)


def _gqa_attention(q, k, v, batch, seq):
    ncols = B_Q_WIDTH // LANES
    tq = min(ROWS_ATTN // (2 * ncols), seq)
    tk = min(TK_ATTN, seq)
    nk = seq // tk
    rows = 2 * ncols * tq
    nq = seq // tq
    return pl.pallas_call(
        functools.partial(_gqa_kernel, tq=tq, tk=tk),
        grid=(batch, nq),
        in_specs=[
            _SMEM_SPEC,
            pl.BlockSpec((tq, B_Q_WIDTH), lambda b, i: (b * nq + i, 0)),
            pl.BlockSpec((seq, B_KV_WIDTH), lambda b, i: (b, 0)),
            pl.BlockSpec((seq, B_KV_WIDTH), lambda b, i: (b, 0)),
        ],
        out_specs=pl.BlockSpec((tq, B_Q_WIDTH), lambda b, i: (b * nq + i, 0)),
        out_shape=jax.ShapeDtypeStruct(q.shape, BF16),
        scratch_shapes=_attn_scratch(rows, nk, tk, seq),
        compiler_params=_params("parallel", "arbitrary"),
        name="gqa_attention",
    )(_trip_count(nk), q, k, v)


def _diff_kernel(nk_ref, q_ref, k_ref, v_ref, lq1_ref, lk1_ref, lq2_ref, lk2_ref, g_ref, o_ref,
                 qm_ref, s_ref, m_ref, vext_ref, acc_ref, *, tq, tk, lam_init):
    _stack_masked_heads(q_ref, qm_ref, 1, tq)
    o = _softmax_pv(pl.program_id(2) == 0, qm_ref, k_ref, v_ref, s_ref, m_ref, vext_ref, acc_ref, nk_ref[0], tk)
    lam = (jnp.exp(jnp.sum(lq1_ref[...] * lk1_ref[...], axis=-1, keepdims=True))
           - jnp.exp(jnp.sum(lq2_ref[...] * lk2_ref[...], axis=-1, keepdims=True)) + lam_init)
    d = o[0:tq] - lam * o[tq:2 * tq]
    o_ref[...] = (_rmsnorm_rows(d, g_ref[...], SUBLN_EPS) * (1.0 - lam_init)).astype(BF16)


def _diff_attention(q, k, v, lq1, lk1, lq2, lk2, g, batch, seq, lam_init):
    tq = min(ROWS_ATTN // 2, seq)
    tk = min(TK_ATTN, seq)
    nk = seq // tk
    rows = 2 * tq
    nq = seq // tq
    vec = pl.BlockSpec((1, HEAD_DIM), lambda b, h, i: (0, 0))
    return pl.pallas_call(
        functools.partial(_diff_kernel, tq=tq, tk=tk, lam_init=lam_init),
        grid=(batch, DIFF_HEADS, nq),
        in_specs=[
            _SMEM_SPEC,
            pl.BlockSpec((tq, LANES), lambda b, h, i: (b * nq + i, h)),
            pl.BlockSpec((seq, LANES), lambda b, h, i: (b, h)),
            pl.BlockSpec((seq, LANES), lambda b, h, i: (b, h)),
            vec, vec, vec, vec,
            pl.BlockSpec((1, LANES), lambda b, h, i: (0, 0)),
        ],
        out_specs=pl.BlockSpec((tq, LANES), lambda b, h, i: (b * nq + i, h)),
        out_shape=jax.ShapeDtypeStruct(q.shape, BF16),
        scratch_shapes=_attn_scratch(rows, nk, tk, seq),
        compiler_params=_params("parallel", "parallel", "arbitrary"),
        name="diff_attention",
    )(_trip_count(nk), q, k, v, lq1, lk1, lq2, lk2, g)


def _na_band_start(n):
    return int(np.clip(n * NA_QCOLS - NA_WIN_COLS // 2, 0, GRID_W - NA_KCOLS))


def _na_bias_indices():
    qr = np.arange(NA_QROWS)[:, None]
    kr = np.arange(NA_KROWS)[None, :]
    half = NA_WIN_ROWS // 2
    key_shift = [0, -half, -(NA_KROWS - NA_QROWS)]
    win_start = [np.maximum(qr - half, 0), qr, np.minimum(qr + half, NA_KROWS - NA_WIN_ROWS)]
    drow = np.stack([np.clip(kr + shift - qr + NA_WIN_ROWS - 1, 0, 2 * NA_WIN_ROWS - 2) for shift in key_shift])
    rmask = np.stack([(kr >= w0) & (kr < w0 + NA_WIN_ROWS) for w0 in win_start])
    qc = np.arange(NA_QCOLS)[:, None]
    kc = np.arange(NA_KCOLS)[None, :]
    dcol, cmask = [], []
    for n in range(GRID_W // NA_QCOLS):
        qabs = n * NA_QCOLS + qc
        kabs = _na_band_start(n) + kc
        w0 = np.clip(qabs - NA_WIN_COLS // 2, 0, GRID_W - NA_WIN_COLS)
        dcol.append(np.clip(kabs - qabs + NA_WIN_COLS - 1, 0, 2 * NA_WIN_COLS - 2))
        cmask.append((kabs >= w0) & (kabs < w0 + NA_WIN_COLS))
    return drow, rmask, np.stack(dcol), np.stack(cmask)


def _na_bias_table(rpb):
    drow, rmask, dcol, cmask = _na_bias_indices()
    oh_r = jnp.asarray(np.eye(2 * NA_WIN_ROWS - 1, dtype=np.float32)[drow])
    oh_c = jnp.asarray(np.eye(2 * NA_WIN_COLS - 1, dtype=np.float32)[dcol])
    hp = lax.Precision.HIGHEST
    a = jnp.einsum("hab,vqka->hvqkb", rpb.astype(F32), oh_r, precision=hp)
    t = jnp.einsum("hvqkb,nxyb->vnhqxky", a, oh_c, precision=hp)
    mask = rmask[:, None, None, :, None, :, None] & cmask[None, :, None, None, :, None, :]
    t = jnp.where(jnp.asarray(mask), t * LOG2E, -jnp.inf)
    return t.reshape(t.shape[:3] + (NA_QROWS * NA_QCOLS, NA_KROWS * NA_KCOLS))


def _na_kernel(q_ref, k0, k1, k2, k3, v0, v1, v2, v3, tbl_ref, o_ref):
    kps = (k0, k1, k2, k3)
    vps = (v0, v1, v2, v3)
    lo = _lane_lo_mask()
    nq = NA_QROWS * NA_QCOLS
    for n in range(GRID_W // NA_QCOLS):
        band = _na_band_start(n)
        for c in range(A_WIDTH // LANES):
            lanes = slice(c * LANES, (c + 1) * LANES)
            qs = jnp.concatenate(
                [q_ref[r * GRID_W + n * NA_QCOLS:r * GRID_W + (n + 1) * NA_QCOLS, lanes] for r in range(NA_QROWS)],
                axis=0)
            zero = jnp.zeros_like(qs)
            qm = jnp.concatenate([jnp.where(lo, qs, zero), jnp.where(lo, zero, qs)], axis=0)

            def band_rows(pieces):
                return jnp.concatenate(
                    [p[r * GRID_W + band:r * GRID_W + band + NA_KCOLS, lanes]
                     for p in pieces for r in range(NA_KPIECE)], axis=0).astype(BF16)

            kb = band_rows(kps)
            vb = band_rows(vps)
            s = _dot_nt(qm, kb)
            s = s + jnp.concatenate([tbl_ref[0, n, 2 * c], tbl_ref[0, n, 2 * c + 1]], axis=0)
            m = jnp.max(s, axis=-1, keepdims=True)
            p = jnp.exp2(s - m)
            l = jnp.sum(p, axis=-1, keepdims=True)
            o = _dot(p.astype(BF16), vb) / l
            res = jnp.where(lo, o[0:nq], o[nq:2 * nq]).astype(BF16)
            for r in range(NA_QROWS):
                o_ref[r * GRID_W + n * NA_QCOLS:r * GRID_W + (n + 1) * NA_QCOLS, lanes] = (
                    res[r * NA_QCOLS:(r + 1) * NA_QCOLS])


def _na_attention(q, k, v, tbl, batch, seq):
    grid_rows = seq // GRID_W
    steps = grid_rows // NA_QROWS
    tq = NA_QROWS * GRID_W
    tp = NA_KPIECE * GRID_W
    pieces_per_batch = grid_rows // NA_KPIECE
    npieces = NA_KROWS // NA_KPIECE
    shift = (NA_WIN_ROWS // 2) // NA_KPIECE

    def piece_map(i):
        def index(b, t):
            start = jnp.clip(t * (NA_QROWS // NA_KPIECE) - shift, 0, pieces_per_batch - npieces)
            return (b * pieces_per_batch + start + i, 0)
        return index

    def tbl_map(b, t):
        variant = jnp.where(t == 0, 0, jnp.where(t == steps - 1, 2, 1))
        return (variant, 0, 0, 0, 0)

    piece_specs = [pl.BlockSpec((tp, A_WIDTH), piece_map(i)) for i in range(npieces)]
    return pl.pallas_call(
        _na_kernel,
        grid=(batch, steps),
        in_specs=[pl.BlockSpec((tq, A_WIDTH), lambda b, t: (b * steps + t, 0))] + piece_specs + piece_specs + [
            pl.BlockSpec((1,) + tbl.shape[1:], tbl_map)],
        out_specs=pl.BlockSpec((tq, A_WIDTH), lambda b, t: (b * steps + t, 0)),
        out_shape=jax.ShapeDtypeStruct(q.shape, BF16),
        compiler_params=_params("parallel", "arbitrary"),
        name="na_attention",
    )(q, k, k, k, k, v, v, v, v, tbl)


def _mlp_kernel(*refs, n_attn, final):
    x_ref = refs[0]
    attn_refs = refs[1:1 + n_attn]
    wo_ref, g_ref, wup_ref, wdown_ref, gf_ref, o_ref, h_ref, acc_ref = refs[1 + n_attn:]
    j = pl.program_id(1)

    @pl.when(j == 0)
    def _():
        x = x_ref[...]
        width = wo_ref.shape[0] // n_attn
        for i, a_ref in enumerate(attn_refs):
            x = x + _dot(a_ref[...], wo_ref[i * width:(i + 1) * width, :])
        acc_ref[...] = x
        h_ref[...] = _rmsnorm_rows(x, g_ref[...], NORM_EPS).astype(BF16)

    u = jnp.maximum(_dot(h_ref[...], wup_ref[...]), 0.0)
    acc_ref[...] += _dot((u * u).astype(BF16), wdown_ref[...])

    @pl.when(j == pl.num_programs(1) - 1)
    def _():
        y = acc_ref[...]
        if final:
            y = _rmsnorm_rows(y, gf_ref[...], NORM_EPS)
        o_ref[...] = y


def _outproj_mlp(x, attn, wo, g, wup, wdown, gf, final):
    n = x.shape[0]
    tm = min(TM_MLP, n)
    tf = TF_MLP
    row = lambda i, j: (i, 0)
    const = lambda i, j: (0, 0)
    return pl.pallas_call(
        functools.partial(_mlp_kernel, n_attn=len(attn), final=final),
        grid=(n // tm, D_FF // tf),
        in_specs=[pl.BlockSpec((tm, D_MODEL), row)]
        + [pl.BlockSpec((tm, a.shape[1]), row) for a in attn]
        + [
            pl.BlockSpec(wo.shape, const),
            pl.BlockSpec((1, D_MODEL), const),
            pl.BlockSpec((D_MODEL, tf), lambda i, j: (0, j)),
            pl.BlockSpec((tf, D_MODEL), lambda i, j: (j, 0)),
            pl.BlockSpec((1, D_MODEL), const),
        ],
        out_specs=pl.BlockSpec((tm, D_MODEL), row),
        out_shape=jax.ShapeDtypeStruct(x.shape, F32),
        scratch_shapes=[pltpu.VMEM((tm, D_MODEL), BF16), pltpu.VMEM((tm, D_MODEL), F32)],
        compiler_params=_params("parallel", "arbitrary"),
        name="outproj_mlp",
    )(x, *attn, wo, g, wup, wdown, gf)


def _rope_angles(pos, dim, theta):
    inv_freq = 1.0 / jnp.power(theta, jnp.arange(0, dim, 2, dtype=F32) / dim)
    ang = pos.astype(F32)[:, None] * inv_freq[None, :]
    return jnp.cos(ang), jnp.sin(ang)


def _axial_tables(seq):
    t = jnp.arange(seq)
    half = HEAD_DIM // 2
    cr, sr = _rope_angles(t // GRID_W, half, AXIAL_THETA)
    cc, sc = _rope_angles(t % GRID_W, half, AXIAL_THETA)
    cos = jnp.concatenate([cr, cr, cc, cc], axis=-1)
    sin = jnp.concatenate([sr, sr, sc, sc], axis=-1)
    return jnp.tile(cos, (1, LANES // HEAD_DIM)), jnp.tile(sin, (1, LANES // HEAD_DIM))


def _rope_tables(seq):
    c, s = _rope_angles(jnp.arange(seq), HEAD_DIM, ROPE_THETA)
    cos = jnp.concatenate([c, c], axis=-1)
    sin = jnp.concatenate([s, s], axis=-1)
    return jnp.tile(cos, (1, LANES // HEAD_DIM)), jnp.tile(sin, (1, LANES // HEAD_DIM))


_q = HEAD_DIM // 4
_AXIAL_SRC = np.concatenate([np.arange(_q, 2 * _q), np.arange(0, _q), np.arange(3 * _q, 4 * _q), np.arange(2 * _q, 3 * _q)])
_AXIAL_SIGN = np.concatenate([-np.ones(_q), np.ones(_q), -np.ones(_q), np.ones(_q)]).astype(np.float32)
_h = HEAD_DIM // 2
_ROPE_SRC = np.concatenate([np.arange(_h, 2 * _h), np.arange(0, _h)])
_ROPE_SIGN = np.concatenate([-np.ones(_h), np.ones(_h)]).astype(np.float32)
_GQA_HEAD_ORDER = np.arange(GQA_Q_HEADS).reshape(GQA_KV_HEADS, -1).T.reshape(-1)


def _rotated_columns(w, heads, src, sign):
    w3 = w.reshape(w.shape[0], heads, HEAD_DIM)
    return (w3[:, :, src] * sign).reshape(w.shape)


def _head_columns(w, order):
    w3 = w.reshape(w.shape[0], -1, HEAD_DIM)
    return w3[:, order, :].reshape(w.shape)


def _prepare_even(w_in, q_norm, k_norm, w_out):
    a, bq, bkv = A_WIDTH, B_Q_WIDTH, B_KV_WIDTH
    wqa, wka, wva = w_in[:, 0:a], w_in[:, a:2 * a], w_in[:, 2 * a:3 * a]
    wqb = _head_columns(w_in[:, 3 * a:3 * a + bq], _GQA_HEAD_ORDER)
    wkb = w_in[:, 3 * a + bq:3 * a + bq + bkv]
    wvb = w_in[:, 3 * a + bq + bkv:]
    w = jnp.concatenate([
        wqa, wka, wva,
        wqb, _rotated_columns(wqb, GQA_Q_HEADS, _AXIAL_SRC, _AXIAL_SIGN),
        wkb, _rotated_columns(wkb, GQA_KV_HEADS, _AXIAL_SRC, _AXIAL_SIGN),
        wvb], axis=1).astype(BF16)
    reps = LANES // HEAD_DIM
    gq = jnp.tile(q_norm, reps)[None]
    gqr = jnp.tile(q_norm[_AXIAL_SRC], reps)[None]
    gk = jnp.tile(k_norm, reps)[None]
    gkr = jnp.tile(k_norm[_AXIAL_SRC], reps)[None]
    wo_b = w_out[a:].reshape(GQA_Q_HEADS, HEAD_DIM, -1)[_GQA_HEAD_ORDER].reshape(bq, -1)
    wo = jnp.concatenate([w_out[:a], wo_b], axis=0).astype(BF16)
    return w, gq, gqr, gk, gkr, wo


def _prepare_odd(w_in):
    d = DIFF_WIDTH
    wq, wk, wv = w_in[:, 0:d], w_in[:, d:2 * d], w_in[:, 2 * d:]
    heads = d // HEAD_DIM
    return jnp.concatenate([
        wq, _rotated_columns(wq, heads, _ROPE_SRC, _ROPE_SIGN),
        wk, _rotated_columns(wk, heads, _ROPE_SRC, _ROPE_SIGN),
        wv], axis=1).astype(BF16)


def _group_ones():
    g = np.arange(LANES) // HEAD_DIM
    return jnp.asarray((g[:, None] == g[None, :]).astype(np.float32), dtype=BF16)


def _trunk(x3, p):
    batch, seq, _ = x3.shape
    assert seq % (NA_KROWS * GRID_W) == 0 and seq % min(TK_ATTN, seq) == 0
    x = x3.reshape(batch * seq, D_MODEL)
    depth = len(p["layers"])
    for layer, lp in enumerate(p["layers"]):
        final = layer == depth - 1
        if layer % 2 == 0:
            cos, sin = _axial_tables(seq)
            qa, ka, va, qb, kb, vb = _inproj_even(x, lp["ln_mix"], lp["w_in"], p["gmat"], lp["gq"], lp["gqr"],
                                                  lp["gk"], lp["gkr"], cos, sin, seq)
            a_out = _na_attention(qa, ka, va, lp["na_bias"], batch, seq)
            b_out = _gqa_attention(qb, kb, vb, batch, seq)
            attn = [a_out, b_out]
        else:
            cos, sin = _rope_tables(seq)
            q, k, v = _inproj_odd(x, lp["ln_mix"], lp["w_in"], cos, sin, seq)
            attn = [_diff_attention(q, k, v, lp["lq1"], lp["lk1"], lp["lq2"], lp["lk2"], lp["subln"],
                                    batch, seq, lp["lam_init"])]
        x = _outproj_mlp(x, attn, lp["w_out"], lp["ln_mlp"], lp["w_up"], lp["w_down"], p["ln_f"], final)
    return x.reshape(batch, seq, D_MODEL)


def kernel(x_prompt, x_sample, ln_mix_e, w_in_e, rpb, q_norm_b, k_norm_b, w_out_e, ln_mix_o, w_in_o, lambda_q1, lambda_k1, lambda_q2, lambda_k2, subln_g, w_out_o, ln_mlp, w_up, w_down, ln_f):
    depth = ln_mlp.shape[0]
    layers = []
    for layer in range(depth):
        j = layer // 2
        lp = {
            "ln_mlp": ln_mlp[layer][None],
            "w_up": w_up[layer].astype(BF16),
            "w_down": w_down[layer].astype(BF16),
        }
        if layer % 2 == 0:
            w, gq, gqr, gk, gkr, wo = _prepare_even(w_in_e[j], q_norm_b[j], k_norm_b[j], w_out_e[j])
            lp.update(ln_mix=ln_mix_e[j][None], w_in=w, gq=gq, gqr=gqr, gk=gk, gkr=gkr, w_out=wo,
                      na_bias=_na_bias_table(rpb[j]))
        else:
            lp.update(ln_mix=ln_mix_o[j][None], w_in=_prepare_odd(w_in_o[j]), w_out=w_out_o[j].astype(BF16),
                      lq1=lambda_q1[j][None], lk1=lambda_k1[j][None], lq2=lambda_q2[j][None],
                      lk2=lambda_k2[j][None], subln=subln_g[j][None],
                      lam_init=0.8 - 0.6 * math.exp(-0.3 * layer))
        layers.append(lp)
    p = {"layers": layers, "gmat": _group_ones(), "ln_f": ln_f[None]}
    return (_trunk(x_prompt, p), _trunk(x_sample, p))
```

```python
import functools
import math

import numpy as np
import jax
import jax.numpy as jnp
from jax import lax
from jax.experimental import pallas as pl
from jax.experimental.pallas import tpu as pltpu

F32 = jnp.float32
BF16 = jnp.bfloat16

D_MODEL = 1024
HEAD_DIM = 64
GRID_W = 64
NA_HEADS = 8
NA_WIN_ROWS = 8
NA_WIN_COLS = 16
NA_QCOLS = 16
NA_KCOLS = 32
GQA_Q_HEADS = 8
GQA_KV_HEADS = 2
AXIAL_THETA = 10000.0
DIFF_HEADS = 8
D_FF = 4 * D_MODEL
ROPE_THETA = 10000.0
NORM_EPS = 1e-6
QK_NORM_EPS = 1e-6
SUBLN_EPS = 1e-5
A_WIDTH = NA_HEADS * HEAD_DIM
B_Q_WIDTH = GQA_Q_HEADS * HEAD_DIM
B_KV_WIDTH = GQA_KV_HEADS * HEAD_DIM
DIFF_WIDTH = 2 * DIFF_HEADS * HEAD_DIM
SM_SCALE = HEAD_DIM ** -0.5
LOG2E = math.log2(math.e)
Q_SCALE = SM_SCALE * LOG2E

LANES = 128
VMEM_LIMIT = 56 * 1024 * 1024

TM_PROJ = 512
TM_MLP = 1024
TF_MLP = 1024
ROWS_ATTN = 2048
ROWS_PV = 1024
NA_QROWS = 8
NA_KROWS = 16
NA_KPIECE = 4


def _params(*sem):
    return pltpu.CompilerParams(dimension_semantics=sem, vmem_limit_bytes=VMEM_LIMIT)


def _rmsnorm_rows(x, g, eps):
    ms = jnp.mean(x * x, axis=-1, keepdims=True)
    return x * lax.rsqrt(ms + eps) * g


def _dot(a, b):
    return jnp.dot(a, b, preferred_element_type=F32)


def _dot_nt(a, b):
    return lax.dot_general(a, b, (((1,), (1,)), ((), ())), preferred_element_type=F32)


def _lane_lo_mask():
    return lax.broadcasted_iota(jnp.int32, (1, LANES), 1) < HEAD_DIM


def _swap_halves(x, span):
    first = lax.broadcasted_iota(jnp.int32, (1, LANES), 1) % (2 * span) < span
    return jnp.where(first, pltpu.roll(x, LANES - span, 1), pltpu.roll(x, span, 1))


def _group_sumsq(x, gmat_ref):
    sq = x * x
    hi = sq.astype(BF16)
    lo = (sq - hi.astype(F32)).astype(BF16)
    return _dot(hi, gmat_ref[...]) + _dot(lo, gmat_ref[...])


def _inproj_even_kernel(x_ref, g_ref, w_ref, gmat_ref, gq_ref, gk_ref, cos_ref, sin_ref,
                        qa_ref, ka_ref, va_ref, qb_ref, kb_ref, vb_ref):
    h = _rmsnorm_rows(x_ref[...], g_ref[...], NORM_EPS).astype(BF16)
    a, bq, bkv = A_WIDTH, B_Q_WIDTH, B_KV_WIDTH
    qa_ref[...] = (_dot(h, w_ref[:, 0:a]) * Q_SCALE).astype(BF16)
    ka_ref[...] = _dot(h, w_ref[:, a:2 * a])
    va_ref[...] = _dot(h, w_ref[:, 2 * a:3 * a])
    o = 3 * a
    q = _dot(h, w_ref[:, o:o + bq])
    cos = cos_ref[...]
    sin = sin_ref[...]

    def norm_rope(x, g):
        r = lax.rsqrt(_group_sumsq(x, gmat_ref) * (1.0 / HEAD_DIM) + QK_NORM_EPS)
        xg = x * g
        return r * (xg * cos + _swap_halves(xg, HEAD_DIM // 4) * sin)

    for c in range(bq // LANES):
        sl = slice(c * LANES, (c + 1) * LANES)
        qb_ref[:, sl] = (norm_rope(q[:, sl], gq_ref[...]) * Q_SCALE).astype(BF16)
    o += bq
    kb_ref[...] = norm_rope(_dot(h, w_ref[:, o:o + bkv]), gk_ref[...]).astype(BF16)
    o += bkv
    vb_ref[...] = _dot(h, w_ref[:, o:o + bkv]).astype(BF16)


def _inproj_even(x, g, w, gmat, gq, gk, cos, sin, seq):
    n = x.shape[0]
    tm = min(TM_PROJ, seq)
    pos_blocks = seq // tm
    row = lambda i: (i, 0)
    const = lambda i: (0, 0)
    pos = lambda i: (i % pos_blocks, 0)
    wcols = w.shape[1]
    return pl.pallas_call(
        _inproj_even_kernel,
        grid=(n // tm,),
        in_specs=[
            pl.BlockSpec((tm, D_MODEL), row),
            pl.BlockSpec((1, D_MODEL), const),
            pl.BlockSpec((D_MODEL, wcols), const),
            pl.BlockSpec((LANES, LANES), const),
            pl.BlockSpec((1, LANES), const),
            pl.BlockSpec((1, LANES), const),
            pl.BlockSpec((tm, LANES), pos),
            pl.BlockSpec((tm, LANES), pos),
        ],
        out_specs=[
            pl.BlockSpec((tm, A_WIDTH), row),
            pl.BlockSpec((tm, A_WIDTH), row),
            pl.BlockSpec((tm, A_WIDTH), row),
            pl.BlockSpec((tm, B_Q_WIDTH), row),
            pl.BlockSpec((tm, B_KV_WIDTH), row),
            pl.BlockSpec((tm, B_KV_WIDTH), row),
        ],
        out_shape=[
            jax.ShapeDtypeStruct((n, A_WIDTH), BF16),
            jax.ShapeDtypeStruct((n, A_WIDTH), F32),
            jax.ShapeDtypeStruct((n, A_WIDTH), F32),
            jax.ShapeDtypeStruct((n, B_Q_WIDTH), BF16),
            jax.ShapeDtypeStruct((n, B_KV_WIDTH), BF16),
            jax.ShapeDtypeStruct((n, B_KV_WIDTH), BF16),
        ],
        compiler_params=_params("parallel"),
        name="inproj_even",
    )(x, g, w, gmat, gq, gk, cos, sin)


def _inproj_odd_kernel(x_ref, g_ref, w_ref, cos_ref, sin_ref, q_ref, k_ref, v_ref):
    h = _rmsnorm_rows(x_ref[...], g_ref[...], NORM_EPS).astype(BF16)
    d = DIFF_WIDTH
    cos = cos_ref[...]
    sin = sin_ref[...]

    def rope(o, scale, out_ref):
        x = _dot(h, w_ref[:, o:o + d])
        for c in range(d // LANES):
            xc = x[:, c * LANES:(c + 1) * LANES]
            out = xc * cos + _swap_halves(xc, HEAD_DIM // 2) * sin
            out_ref[:, c * LANES:(c + 1) * LANES] = (out * scale).astype(BF16)

    rope(0, Q_SCALE, q_ref)
    rope(d, 1.0, k_ref)
    v_ref[...] = _dot(h, w_ref[:, 2 * d:3 * d]).astype(BF16)


def _inproj_odd(x, g, w, cos, sin, seq):
    n = x.shape[0]
    tm = min(TM_PROJ, seq)
    pos_blocks = seq // tm
    row = lambda i: (i, 0)
    const = lambda i: (0, 0)
    pos = lambda i: (i % pos_blocks, 0)
    out = jax.ShapeDtypeStruct((n, DIFF_WIDTH), BF16)
    return pl.pallas_call(
        _inproj_odd_kernel,
        grid=(n // tm,),
        in_specs=[
            pl.BlockSpec((tm, D_MODEL), row),
            pl.BlockSpec((1, D_MODEL), const),
            pl.BlockSpec((D_MODEL, w.shape[1]), const),
            pl.BlockSpec((tm, LANES), pos),
            pl.BlockSpec((tm, LANES), pos),
        ],
        out_specs=[pl.BlockSpec((tm, DIFF_WIDTH), row)] * 3,
        out_shape=[out, out, out],
        compiler_params=_params("parallel"),
        name="inproj_odd",
    )(x, g, w, cos, sin)


def _stack_masked_heads(q_ref, qm_ref, ncols, tq):
    lo = _lane_lo_mask()
    for c in range(ncols):
        qc = q_ref[:, c * LANES:(c + 1) * LANES]
        zero = jnp.zeros_like(qc)
        qm_ref[2 * c * tq:(2 * c + 1) * tq, :] = jnp.where(lo, qc, zero)
        qm_ref[(2 * c + 1) * tq:(2 * c + 2) * tq, :] = jnp.where(lo, zero, qc)


def _softmax_pv(first_tile, trips_ref, qm_ref, k_ref, v_ref, s_ref, m_ref, vext_ref, acc_ref):
    seq = k_ref.shape[0]
    rb = min(ROWS_PV, qm_ref.shape[0])

    @pl.when(first_tile)
    def _():
        vext_ref[:, 0:LANES] = v_ref[...]
        vext_ref[:, LANES:2 * LANES] = jnp.ones(v_ref.shape, BF16)

    def scores(j, carry):
        s = _dot_nt(qm_ref[...], k_ref[...])
        s_ref[...] = s
        m = s[:, 0:LANES]
        for i in range(1, seq // LANES):
            m = jnp.maximum(m, s[:, i * LANES:(i + 1) * LANES])
        m_ref[...] = m
        return carry

    lax.fori_loop(0, trips_ref[0], scores, 0)
    m_ref[...] = jnp.broadcast_to(jnp.max(m_ref[...], axis=-1, keepdims=True), m_ref.shape)

    def values(r, carry):
        rows = pl.ds(pl.multiple_of(r * rb, rb), rb)
        s = s_ref[rows, :]
        m = m_ref[rows, :]
        ps = [jnp.exp2(s[:, i * LANES:(i + 1) * LANES] - m).astype(BF16) for i in range(seq // LANES)]
        acc_ref[rows, :] = _dot(jnp.concatenate(ps, axis=1), vext_ref[...])
        return carry

    lax.fori_loop(0, trips_ref[1], values, 0)
    return acc_ref[:, 0:LANES] / acc_ref[:, LANES:2 * LANES]


_SMEM_SPEC = pl.BlockSpec(memory_space=pltpu.SMEM)


def _sweep_trips(rows):
    return jnp.asarray([1, rows // min(ROWS_PV, rows)], jnp.int32)


def _attn_scratch(rows, seq):
    return [
        pltpu.VMEM((rows, LANES), BF16),
        pltpu.VMEM((rows, seq), F32),
        pltpu.VMEM((rows, LANES), F32),
        pltpu.VMEM((seq, 2 * LANES), BF16),
        pltpu.VMEM((rows, 2 * LANES), F32),
    ]


def _gqa_kernel(trips_ref, q_ref, k_ref, v_ref, o_ref, qm_ref, s_ref, m_ref, vext_ref, acc_ref, *, tq):
    ncols = B_Q_WIDTH // LANES
    _stack_masked_heads(q_ref, qm_ref, ncols, tq)
    o = _softmax_pv(pl.program_id(1) == 0, trips_ref, qm_ref, k_ref, v_ref, s_ref, m_ref, vext_ref, acc_ref)
    lo = _lane_lo_mask()
    for c in range(ncols):
        o_lo = o[2 * c * tq:(2 * c + 1) * tq]
        o_hi = o[(2 * c + 1) * tq:(2 * c + 2) * tq]
        o_ref[:, c * LANES:(c + 1) * LANES] = jnp.where(lo, o_lo, o_hi).astype(BF16)


def _gqa_attention(q, k, v, batch, seq):
    ncols = B_Q_WIDTH // LANES
    tq = min(ROWS_ATTN // (2 * ncols), seq)
    rows = 2 * ncols * tq
    nq = seq // tq
    return pl.pallas_call(
        functools.partial(_gqa_kernel, tq=tq),
        grid=(batch, nq),
        in_specs=[
            _SMEM_SPEC,
            pl.BlockSpec((tq, B_Q_WIDTH), lambda b, i: (b * nq + i, 0)),
            pl.BlockSpec((seq, B_KV_WIDTH), lambda b, i: (b, 0)),
            pl.BlockSpec((seq, B_KV_WIDTH), lambda b, i: (b, 0)),
        ],
        out_specs=pl.BlockSpec((tq, B_Q_WIDTH), lambda b, i: (b * nq + i, 0)),
        out_shape=jax.ShapeDtypeStruct(q.shape, BF16),
        scratch_shapes=_attn_scratch(rows, seq),
        compiler_params=_params("parallel", "arbitrary"),
        name="gqa_attention",
    )(_sweep_trips(rows), q, k, v)


def _diff_kernel(trips_ref, q_ref, k_ref, v_ref, lq1_ref, lk1_ref, lq2_ref, lk2_ref, g_ref, o_ref,
                 qm_ref, s_ref, m_ref, vext_ref, acc_ref, *, tq, lam_init):
    _stack_masked_heads(q_ref, qm_ref, 1, tq)
    o = _softmax_pv(pl.program_id(2) == 0, trips_ref, qm_ref, k_ref, v_ref, s_ref, m_ref, vext_ref, acc_ref)
    lam = (jnp.exp(jnp.sum(lq1_ref[...] * lk1_ref[...], axis=-1, keepdims=True))
           - jnp.exp(jnp.sum(lq2_ref[...] * lk2_ref[...], axis=-1, keepdims=True)) + lam_init)
    d = o[0:tq] - lam * o[tq:2 * tq]
    o_ref[...] = (_rmsnorm_rows(d, g_ref[...], SUBLN_EPS) * (1.0 - lam_init)).astype(BF16)


def _diff_attention(q, k, v, lq1, lk1, lq2, lk2, g, batch, seq, lam_init):
    tq = min(ROWS_ATTN // 2, seq)
    rows = 2 * tq
    nq = seq // tq
    vec = pl.BlockSpec((1, HEAD_DIM), lambda b, h, i: (0, 0))
    return pl.pallas_call(
        functools.partial(_diff_kernel, tq=tq, lam_init=lam_init),
        grid=(batch, DIFF_HEADS, nq),
        in_specs=[
            _SMEM_SPEC,
            pl.BlockSpec((tq, LANES), lambda b, h, i: (b * nq + i, h)),
            pl.BlockSpec((seq, LANES), lambda b, h, i: (b, h)),
            pl.BlockSpec((seq, LANES), lambda b, h, i: (b, h)),
            vec, vec, vec, vec,
            pl.BlockSpec((1, LANES), lambda b, h, i: (0, 0)),
        ],
        out_specs=pl.BlockSpec((tq, LANES), lambda b, h, i: (b * nq + i, h)),
        out_shape=jax.ShapeDtypeStruct(q.shape, BF16),
        scratch_shapes=_attn_scratch(rows, seq),
        compiler_params=_params("parallel", "parallel", "arbitrary"),
        name="diff_attention",
    )(_sweep_trips(rows), q, k, v, lq1, lk1, lq2, lk2, g)


def _na_band_start(n):
    return int(np.clip(n * NA_QCOLS - NA_WIN_COLS // 2, 0, GRID_W - NA_KCOLS))


def _na_bias_indices():
    qr = np.arange(NA_QROWS)[:, None]
    kr = np.arange(NA_KROWS)[None, :]
    half = NA_WIN_ROWS // 2
    key_shift = [0, -half, -(NA_KROWS - NA_QROWS)]
    win_start = [np.maximum(qr - half, 0), qr, np.minimum(qr + half, NA_KROWS - NA_WIN_ROWS)]
    drow = np.stack([np.clip(kr + shift - qr + NA_WIN_ROWS - 1, 0, 2 * NA_WIN_ROWS - 2) for shift in key_shift])
    rmask = np.stack([(kr >= w0) & (kr < w0 + NA_WIN_ROWS) for w0 in win_start])
    qc = np.arange(NA_QCOLS)[:, None]
    kc = np.arange(NA_KCOLS)[None, :]
    dcol, cmask = [], []
    for n in range(GRID_W // NA_QCOLS):
        qabs = n * NA_QCOLS + qc
        kabs = _na_band_start(n) + kc
        w0 = np.clip(qabs - NA_WIN_COLS // 2, 0, GRID_W - NA_WIN_COLS)
        dcol.append(np.clip(kabs - qabs + NA_WIN_COLS - 1, 0, 2 * NA_WIN_COLS - 2))
        cmask.append((kabs >= w0) & (kabs < w0 + NA_WIN_COLS))
    return drow, rmask, np.stack(dcol), np.stack(cmask)


def _na_bias_table(rpb):
    drow, rmask, dcol, cmask = _na_bias_indices()
    oh_r = jnp.asarray(np.eye(2 * NA_WIN_ROWS - 1, dtype=np.float32)[drow])
    oh_c = jnp.asarray(np.eye(2 * NA_WIN_COLS - 1, dtype=np.float32)[dcol])
    hp = lax.Precision.HIGHEST
    a = jnp.einsum("hab,vqka->hvqkb", rpb.astype(F32), oh_r, precision=hp)
    t = jnp.einsum("hvqkb,nxyb->vnhqxky", a, oh_c, precision=hp)
    mask = rmask[:, None, None, :, None, :, None] & cmask[None, :, None, None, :, None, :]
    t = jnp.where(jnp.asarray(mask), t * LOG2E, -jnp.inf)
    return t.reshape(t.shape[:3] + (NA_QROWS * NA_QCOLS, NA_KROWS * NA_KCOLS))


def _na_kernel(q_ref, k0, k1, k2, k3, v0, v1, v2, v3, tbl_ref, o_ref):
    kps = (k0, k1, k2, k3)
    vps = (v0, v1, v2, v3)
    lo = _lane_lo_mask()
    nq = NA_QROWS * NA_QCOLS
    for n in range(GRID_W // NA_QCOLS):
        band = _na_band_start(n)
        for c in range(A_WIDTH // LANES):
            lanes = slice(c * LANES, (c + 1) * LANES)
            qs = jnp.concatenate(
                [q_ref[r * GRID_W + n * NA_QCOLS:r * GRID_W + (n + 1) * NA_QCOLS, lanes] for r in range(NA_QROWS)],
                axis=0)
            zero = jnp.zeros_like(qs)
            qm = jnp.concatenate([jnp.where(lo, qs, zero), jnp.where(lo, zero, qs)], axis=0)

            def band_rows(pieces):
                return jnp.concatenate(
                    [p[r * GRID_W + band:r * GRID_W + band + NA_KCOLS, lanes]
                     for p in pieces for r in range(NA_KPIECE)], axis=0).astype(BF16)

            kb = band_rows(kps)
            vb = band_rows(vps)
            s = _dot_nt(qm, kb)
            s = s + jnp.concatenate([tbl_ref[0, n, 2 * c], tbl_ref[0, n, 2 * c + 1]], axis=0)
            m = jnp.max(s, axis=-1, keepdims=True)
            p = jnp.exp2(s - m)
            l = jnp.sum(p, axis=-1, keepdims=True)
            o = _dot(p.astype(BF16), vb) / l
            res = jnp.where(lo, o[0:nq], o[nq:2 * nq]).astype(BF16)
            for r in range(NA_QROWS):
                o_ref[r * GRID_W + n * NA_QCOLS:r * GRID_W + (n + 1) * NA_QCOLS, lanes] = (
                    res[r * NA_QCOLS:(r + 1) * NA_QCOLS])


def _na_attention(q, k, v, tbl, batch, seq):
    grid_rows = seq // GRID_W
    steps = grid_rows // NA_QROWS
    tq = NA_QROWS * GRID_W
    tp = NA_KPIECE * GRID_W
    pieces_per_batch = grid_rows // NA_KPIECE
    npieces = NA_KROWS // NA_KPIECE
    shift = (NA_WIN_ROWS // 2) // NA_KPIECE

    def piece_map(i):
        def index(b, t):
            start = jnp.clip(t * (NA_QROWS // NA_KPIECE) - shift, 0, pieces_per_batch - npieces)
            return (b * pieces_per_batch + start + i, 0)
        return index

    def tbl_map(b, t):
        variant = jnp.where(t == 0, 0, jnp.where(t == steps - 1, 2, 1))
        return (variant, 0, 0, 0, 0)

    piece_specs = [pl.BlockSpec((tp, A_WIDTH), piece_map(i)) for i in range(npieces)]
    return pl.pallas_call(
        _na_kernel,
        grid=(batch, steps),
        in_specs=[pl.BlockSpec((tq, A_WIDTH), lambda b, t: (b * steps + t, 0))] + piece_specs + piece_specs + [
            pl.BlockSpec((1,) + tbl.shape[1:], tbl_map)],
        out_specs=pl.BlockSpec((tq, A_WIDTH), lambda b, t: (b * steps + t, 0)),
        out_shape=jax.ShapeDtypeStruct(q.shape, BF16),
        compiler_params=_params("parallel", "arbitrary"),
        name="na_attention",
    )(q, k, k, k, k, v, v, v, v, tbl)


def _mlp_kernel(*refs, n_attn, final):
    x_ref = refs[0]
    attn_refs = refs[1:1 + n_attn]
    wo_ref, g_ref, wup_ref, wdown_ref, gf_ref, o_ref, h_ref, acc_ref = refs[1 + n_attn:]
    j = pl.program_id(1)

    @pl.when(j == 0)
    def _():
        x = x_ref[...]
        width = wo_ref.shape[0] // n_attn
        for i, a_ref in enumerate(attn_refs):
            x = x + _dot(a_ref[...], wo_ref[i * width:(i + 1) * width, :])
        acc_ref[...] = x
        h_ref[...] = _rmsnorm_rows(x, g_ref[...], NORM_EPS).astype(BF16)

    u = jnp.maximum(_dot(h_ref[...], wup_ref[...]), 0.0)
    acc_ref[...] += _dot((u * u).astype(BF16), wdown_ref[...])

    @pl.when(j == pl.num_programs(1) - 1)
    def _():
        y = acc_ref[...]
        if final:
            y = _rmsnorm_rows(y, gf_ref[...], NORM_EPS)
        o_ref[...] = y


def _outproj_mlp(x, attn, wo, g, wup, wdown, gf, final):
    n = x.shape[0]
    tm = min(TM_MLP, n)
    tf = TF_MLP
    row = lambda i, j: (i, 0)
    const = lambda i, j: (0, 0)
    return pl.pallas_call(
        functools.partial(_mlp_kernel, n_attn=len(attn), final=final),
        grid=(n // tm, D_FF // tf),
        in_specs=[pl.BlockSpec((tm, D_MODEL), row)]
        + [pl.BlockSpec((tm, a.shape[1]), row) for a in attn]
        + [
            pl.BlockSpec(wo.shape, const),
            pl.BlockSpec((1, D_MODEL), const),
            pl.BlockSpec((D_MODEL, tf), lambda i, j: (0, j)),
            pl.BlockSpec((tf, D_MODEL), lambda i, j: (j, 0)),
            pl.BlockSpec((1, D_MODEL), const),
        ],
        out_specs=pl.BlockSpec((tm, D_MODEL), row),
        out_shape=jax.ShapeDtypeStruct(x.shape, F32),
        scratch_shapes=[pltpu.VMEM((tm, D_MODEL), BF16), pltpu.VMEM((tm, D_MODEL), F32)],
        compiler_params=_params("parallel", "arbitrary"),
        name="outproj_mlp",
    )(x, *attn, wo, g, wup, wdown, gf)


def _rope_angles(pos, dim, theta):
    inv_freq = 1.0 / jnp.power(theta, jnp.arange(0, dim, 2, dtype=F32) / dim)
    ang = pos.astype(F32)[:, None] * inv_freq[None, :]
    return jnp.cos(ang), jnp.sin(ang)


def _axial_tables(seq):
    t = jnp.arange(seq)
    half = HEAD_DIM // 2
    cr, sr = _rope_angles(t // GRID_W, half, AXIAL_THETA)
    cc, sc = _rope_angles(t % GRID_W, half, AXIAL_THETA)
    cos = jnp.concatenate([cr, cr, cc, cc], axis=-1)
    sin = jnp.concatenate([-sr, sr, -sc, sc], axis=-1)
    return jnp.tile(cos, (1, LANES // HEAD_DIM)), jnp.tile(sin, (1, LANES // HEAD_DIM))


def _rope_tables(seq):
    c, s = _rope_angles(jnp.arange(seq), HEAD_DIM, ROPE_THETA)
    cos = jnp.concatenate([c, c], axis=-1)
    sin = jnp.concatenate([-s, s], axis=-1)
    return jnp.tile(cos, (1, LANES // HEAD_DIM)), jnp.tile(sin, (1, LANES // HEAD_DIM))


_GQA_HEAD_ORDER = np.arange(GQA_Q_HEADS).reshape(GQA_KV_HEADS, -1).T.reshape(-1)


def _prepare_even(w_in, q_norm, k_norm, w_out):
    a, bq = A_WIDTH, B_Q_WIDTH
    wqb = w_in[:, 3 * a:3 * a + bq].reshape(w_in.shape[0], GQA_Q_HEADS, HEAD_DIM)[:, _GQA_HEAD_ORDER, :]
    w = jnp.concatenate([w_in[:, :3 * a], wqb.reshape(w_in.shape[0], bq), w_in[:, 3 * a + bq:]], axis=1).astype(BF16)
    reps = LANES // HEAD_DIM
    gq = jnp.tile(q_norm, reps)[None]
    gk = jnp.tile(k_norm, reps)[None]
    wo_b = w_out[a:].reshape(GQA_Q_HEADS, HEAD_DIM, -1)[_GQA_HEAD_ORDER].reshape(bq, -1)
    wo = jnp.concatenate([w_out[:a], wo_b], axis=0).astype(BF16)
    return w, gq, gk, wo


def _group_ones():
    g = np.arange(LANES) // HEAD_DIM
    return jnp.asarray((g[:, None] == g[None, :]).astype(np.float32), dtype=BF16)


def _trunk(x3, p):
    batch, seq, _ = x3.shape
    assert seq % (NA_KROWS * GRID_W) == 0
    x = x3.reshape(batch * seq, D_MODEL)
    depth = len(p["layers"])
    for layer, lp in enumerate(p["layers"]):
        final = layer == depth - 1
        if layer % 2 == 0:
            cos, sin = _axial_tables(seq)
            qa, ka, va, qb, kb, vb = _inproj_even(x, lp["ln_mix"], lp["w_in"], p["gmat"], lp["gq"], lp["gk"],
                                                  cos, sin, seq)
            a_out = _na_attention(qa, ka, va, lp["na_bias"], batch, seq)
            b_out = _gqa_attention(qb, kb, vb, batch, seq)
            attn = [a_out, b_out]
        else:
            cos, sin = _rope_tables(seq)
            q, k, v = _inproj_odd(x, lp["ln_mix"], lp["w_in"], cos, sin, seq)
            attn = [_diff_attention(q, k, v, lp["lq1"], lp["lk1"], lp["lq2"], lp["lk2"], lp["subln"],
                                    batch, seq, lp["lam_init"])]
        x = _outproj_mlp(x, attn, lp["w_out"], lp["ln_mlp"], lp["w_up"], lp["w_down"], p["ln_f"], final)
    return x.reshape(batch, seq, D_MODEL)


def kernel(x_prompt, x_sample, ln_mix_e, w_in_e, rpb, q_norm_b, k_norm_b, w_out_e, ln_mix_o, w_in_o, lambda_q1, lambda_k1, lambda_q2, lambda_k2, subln_g, w_out_o, ln_mlp, w_up, w_down, ln_f):
    depth = ln_mlp.shape[0]
    layers = []
    for layer in range(depth):
        j = layer // 2
        lp = {
            "ln_mlp": ln_mlp[layer][None],
            "w_up": w_up[layer].astype(BF16),
            "w_down": w_down[layer].astype(BF16),
        }
        if layer % 2 == 0:
            w, gq, gk, wo = _prepare_even(w_in_e[j], q_norm_b[j], k_norm_b[j], w_out_e[j])
            lp.update(ln_mix=ln_mix_e[j][None], w_in=w, gq=gq, gk=gk, w_out=wo,
                      na_bias=_na_bias_table(rpb[j]))
        else:
            lp.update(ln_mix=ln_mix_o[j][None], w_in=w_in_o[j].astype(BF16), w_out=w_out_o[j].astype(BF16),
                      lq1=lambda_q1[j][None], lk1=lambda_k1[j][None], lq2=lambda_q2[j][None],
                      lk2=lambda_k2[j][None], subln=subln_g[j][None],
                      lam_init=0.8 - 0.6 * math.exp(-0.3 * layer))
        layers.append(lp)
    p = {"layers": layers, "gmat": _group_ones(), "ln_f": ln_f[None]}
    return (_trunk(x_prompt, p), _trunk(x_sample, p))
```

```python
import functools
import math

import numpy as np
import jax
import jax.numpy as jnp
from jax import lax
from jax.experimental import pallas as pl
from jax.experimental.pallas import tpu as pltpu

F32 = jnp.float32
BF16 = jnp.bfloat16

D_MODEL = 1024
HEAD_DIM = 64
GRID_W = 64
NA_HEADS = 8
NA_WIN_ROWS = 8
NA_WIN_COLS = 16
NA_QCOLS = 16
NA_KCOLS = 32
GQA_Q_HEADS = 8
GQA_KV_HEADS = 2
AXIAL_THETA = 10000.0
DIFF_HEADS = 8
D_FF = 4 * D_MODEL
ROPE_THETA = 10000.0
NORM_EPS = 1e-6
QK_NORM_EPS = 1e-6
SUBLN_EPS = 1e-5
A_WIDTH = NA_HEADS * HEAD_DIM
B_Q_WIDTH = GQA_Q_HEADS * HEAD_DIM
B_KV_WIDTH = GQA_KV_HEADS * HEAD_DIM
DIFF_WIDTH = 2 * DIFF_HEADS * HEAD_DIM
SM_SCALE = HEAD_DIM ** -0.5
LOG2E = math.log2(math.e)
Q_SCALE = SM_SCALE * LOG2E

LANES = 128
VMEM_LIMIT = 56 * 1024 * 1024

TM_PROJ = 1024
TM_MLP = 1024
TF_MLP = 1024
ROWS_ATTN = 2048
ROWS_PV = 1024
NA_QROWS = 8
NA_KROWS = 16
NA_KPIECE = 4


def _params(*sem):
    return pltpu.CompilerParams(dimension_semantics=sem, vmem_limit_bytes=VMEM_LIMIT)


def _rmsnorm_rows(x, g, eps):
    ms = jnp.mean(x * x, axis=-1, keepdims=True)
    return x * lax.rsqrt(ms + eps) * g


def _dot(a, b):
    return jnp.dot(a, b, preferred_element_type=F32)


def _dot_nt(a, b):
    return lax.dot_general(a, b, (((1,), (1,)), ((), ())), preferred_element_type=F32)


def _lane_lo_mask():
    return lax.broadcasted_iota(jnp.int32, (1, LANES), 1) < HEAD_DIM


def _swap_halves(x, span):
    first = lax.broadcasted_iota(jnp.int32, (1, LANES), 1) % (2 * span) < span
    return jnp.where(first, pltpu.roll(x, LANES - span, 1), pltpu.roll(x, span, 1))


def _group_sumsq(x, gmat_ref):
    sq = x * x
    hi = sq.astype(BF16)
    lo = (sq - hi.astype(F32)).astype(BF16)
    return _dot(hi, gmat_ref[...]) + _dot(lo, gmat_ref[...])


def _inproj_even_kernel(x_ref, g_ref, w_ref, gmat_ref, gq_ref, gk_ref, cos_ref, sin_ref,
                        qa_ref, ka_ref, va_ref, qb_ref, kb_ref, vb_ref):
    h = _rmsnorm_rows(x_ref[...], g_ref[...], NORM_EPS).astype(BF16)
    a, bq, bkv = A_WIDTH, B_Q_WIDTH, B_KV_WIDTH
    qa_ref[...] = (_dot(h, w_ref[:, 0:a]) * Q_SCALE).astype(BF16)
    ka_ref[...] = _dot(h, w_ref[:, a:2 * a])
    va_ref[...] = _dot(h, w_ref[:, 2 * a:3 * a])
    o = 3 * a
    q = _dot(h, w_ref[:, o:o + bq])
    cos = cos_ref[...]
    sin = sin_ref[...]

    def norm_rope(x, g):
        r = lax.rsqrt(_group_sumsq(x, gmat_ref) * (1.0 / HEAD_DIM) + QK_NORM_EPS)
        xg = x * g
        return r * (xg * cos + _swap_halves(xg, HEAD_DIM // 4) * sin)

    for c in range(bq // LANES):
        sl = slice(c * LANES, (c + 1) * LANES)
        qb_ref[:, sl] = (norm_rope(q[:, sl], gq_ref[...]) * Q_SCALE).astype(BF16)
    o += bq
    kb_ref[...] = norm_rope(_dot(h, w_ref[:, o:o + bkv]), gk_ref[...]).astype(BF16)
    o += bkv
    vb_ref[...] = _dot(h, w_ref[:, o:o + bkv]).astype(BF16)


def _inproj_even(x, g, w, gmat, gq, gk, cos, sin, seq):
    n = x.shape[0]
    tm = min(TM_PROJ, seq)
    pos_blocks = seq // tm
    row = lambda i: (i, 0)
    const = lambda i: (0, 0)
    pos = lambda i: (i % pos_blocks, 0)
    wcols = w.shape[1]
    return pl.pallas_call(
        _inproj_even_kernel,
        grid=(n // tm,),
        in_specs=[
            pl.BlockSpec((tm, D_MODEL), row),
            pl.BlockSpec((1, D_MODEL), const),
            pl.BlockSpec((D_MODEL, wcols), const),
            pl.BlockSpec((LANES, LANES), const),
            pl.BlockSpec((1, LANES), const),
            pl.BlockSpec((1, LANES), const),
            pl.BlockSpec((tm, LANES), pos),
            pl.BlockSpec((tm, LANES), pos),
        ],
        out_specs=[
            pl.BlockSpec((tm, A_WIDTH), row),
            pl.BlockSpec((tm, A_WIDTH), row),
            pl.BlockSpec((tm, A_WIDTH), row),
            pl.BlockSpec((tm, B_Q_WIDTH), row),
            pl.BlockSpec((tm, B_KV_WIDTH), row),
            pl.BlockSpec((tm, B_KV_WIDTH), row),
        ],
        out_shape=[
            jax.ShapeDtypeStruct((n, A_WIDTH), BF16),
            jax.ShapeDtypeStruct((n, A_WIDTH), F32),
            jax.ShapeDtypeStruct((n, A_WIDTH), F32),
            jax.ShapeDtypeStruct((n, B_Q_WIDTH), BF16),
            jax.ShapeDtypeStruct((n, B_KV_WIDTH), BF16),
            jax.ShapeDtypeStruct((n, B_KV_WIDTH), BF16),
        ],
        compiler_params=_params("parallel"),
        name="inproj_even",
    )(x, g, w, gmat, gq, gk, cos, sin)


def _inproj_odd_kernel(x_ref, g_ref, w_ref, cos_ref, sin_ref, q_ref, k_ref, v_ref):
    h = _rmsnorm_rows(x_ref[...], g_ref[...], NORM_EPS).astype(BF16)
    d = DIFF_WIDTH
    cos = cos_ref[...]
    sin = sin_ref[...]

    def rope(o, scale, out_ref):
        x = _dot(h, w_ref[:, o:o + d])
        for c in range(d // LANES):
            xc = x[:, c * LANES:(c + 1) * LANES]
            out = xc * cos + _swap_halves(xc, HEAD_DIM // 2) * sin
            out_ref[:, c * LANES:(c + 1) * LANES] = (out * scale).astype(BF16)

    rope(0, Q_SCALE, q_ref)
    rope(d, 1.0, k_ref)
    v_ref[...] = _dot(h, w_ref[:, 2 * d:3 * d]).astype(BF16)


def _inproj_odd(x, g, w, cos, sin, seq):
    n = x.shape[0]
    tm = min(TM_PROJ, seq)
    pos_blocks = seq // tm
    row = lambda i: (i, 0)
    const = lambda i: (0, 0)
    pos = lambda i: (i % pos_blocks, 0)
    out = jax.ShapeDtypeStruct((n, DIFF_WIDTH), BF16)
    return pl.pallas_call(
        _inproj_odd_kernel,
        grid=(n // tm,),
        in_specs=[
            pl.BlockSpec((tm, D_MODEL), row),
            pl.BlockSpec((1, D_MODEL), const),
            pl.BlockSpec((D_MODEL, w.shape[1]), const),
            pl.BlockSpec((tm, LANES), pos),
            pl.BlockSpec((tm, LANES), pos),
        ],
        out_specs=[pl.BlockSpec((tm, DIFF_WIDTH), row)] * 3,
        out_shape=[out, out, out],
        compiler_params=_params("parallel"),
        name="inproj_odd",
    )(x, g, w, cos, sin)


def _stack_masked_heads(q_ref, qm_ref, ncols, tq):
    lo = _lane_lo_mask()
    for c in range(ncols):
        qc = q_ref[:, c * LANES:(c + 1) * LANES]
        zero = jnp.zeros_like(qc)
        qm_ref[2 * c * tq:(2 * c + 1) * tq, :] = jnp.where(lo, qc, zero)
        qm_ref[(2 * c + 1) * tq:(2 * c + 2) * tq, :] = jnp.where(lo, zero, qc)


def _softmax_pv(first_tile, trips_ref, qm_ref, k_ref, v_ref, s_ref, m_ref, vext_ref, acc_ref):
    seq = k_ref.shape[0]
    rb = min(ROWS_PV, qm_ref.shape[0])

    @pl.when(first_tile)
    def _():
        vext_ref[:, 0:LANES] = v_ref[...]
        vext_ref[:, LANES:2 * LANES] = jnp.ones(v_ref.shape, BF16)

    def scores(j, carry):
        s = _dot_nt(qm_ref[...], k_ref[...])
        s_ref[...] = s
        m = s[:, 0:LANES]
        for i in range(1, seq // LANES):
            m = jnp.maximum(m, s[:, i * LANES:(i + 1) * LANES])
        m_ref[...] = jnp.broadcast_to(jnp.max(m, axis=-1, keepdims=True), m_ref.shape)
        return carry

    lax.fori_loop(0, trips_ref[0], scores, 0)

    def values(r, carry):
        rows = pl.ds(pl.multiple_of(r * rb, rb), rb)
        s = s_ref[rows, :]
        m = m_ref[rows, :]
        ps = [jnp.exp2(s[:, i * LANES:(i + 1) * LANES] - m).astype(BF16) for i in range(seq // LANES)]
        acc_ref[rows, :] = _dot(jnp.concatenate(ps, axis=1), vext_ref[...])
        return carry

    lax.fori_loop(0, trips_ref[1], values, 0)
    return acc_ref[:, 0:LANES] / acc_ref[:, LANES:2 * LANES]


_SMEM_SPEC = pl.BlockSpec(memory_space=pltpu.SMEM)


def _sweep_trips(rows):
    return jnp.asarray([1, rows // min(ROWS_PV, rows)], jnp.int32)


def _attn_scratch(rows, seq):
    return [
        pltpu.VMEM((rows, LANES), BF16),
        pltpu.VMEM((rows, seq), F32),
        pltpu.VMEM((rows, LANES), F32),
        pltpu.VMEM((seq, 2 * LANES), BF16),
        pltpu.VMEM((rows, 2 * LANES), F32),
    ]


def _gqa_kernel(trips_ref, q_ref, k_ref, v_ref, o_ref, qm_ref, s_ref, m_ref, vext_ref, acc_ref, *, tq):
    ncols = B_Q_WIDTH // LANES
    _stack_masked_heads(q_ref, qm_ref, ncols, tq)
    o = _softmax_pv(pl.program_id(1) == 0, trips_ref, qm_ref, k_ref, v_ref, s_ref, m_ref, vext_ref, acc_ref)
    lo = _lane_lo_mask()
    for c in range(ncols):
        o_lo = o[2 * c * tq:(2 * c + 1) * tq]
        o_hi = o[(2 * c + 1) * tq:(2 * c + 2) * tq]
        o_ref[:, c * LANES:(c + 1) * LANES] = jnp.where(lo, o_lo, o_hi).astype(BF16)


def _gqa_attention(q, k, v, batch, seq):
    ncols = B_Q_WIDTH // LANES
    tq = min(ROWS_ATTN // (2 * ncols), seq)
    rows = 2 * ncols * tq
    nq = seq // tq
    return pl.pallas_call(
        functools.partial(_gqa_kernel, tq=tq),
        grid=(batch, nq),
        in_specs=[
            _SMEM_SPEC,
            pl.BlockSpec((tq, B_Q_WIDTH), lambda b, i: (b * nq + i, 0)),
            pl.BlockSpec((seq, B_KV_WIDTH), lambda b, i: (b, 0)),
            pl.BlockSpec((seq, B_KV_WIDTH), lambda b, i: (b, 0)),
        ],
        out_specs=pl.BlockSpec((tq, B_Q_WIDTH), lambda b, i: (b * nq + i, 0)),
        out_shape=jax.ShapeDtypeStruct(q.shape, BF16),
        scratch_shapes=_attn_scratch(rows, seq),
        compiler_params=_params("parallel", "arbitrary"),
        name="gqa_attention",
    )(_sweep_trips(rows), q, k, v)


def _diff_kernel(trips_ref, q_ref, k_ref, v_ref, lq1_ref, lk1_ref, lq2_ref, lk2_ref, g_ref, o_ref,
                 qm_ref, s_ref, m_ref, vext_ref, acc_ref, *, tq, lam_init):
    _stack_masked_heads(q_ref, qm_ref, 1, tq)
    o = _softmax_pv(pl.program_id(2) == 0, trips_ref, qm_ref, k_ref, v_ref, s_ref, m_ref, vext_ref, acc_ref)
    lam = (jnp.exp(jnp.sum(lq1_ref[...] * lk1_ref[...], axis=-1, keepdims=True))
           - jnp.exp(jnp.sum(lq2_ref[...] * lk2_ref[...], axis=-1, keepdims=True)) + lam_init)
    d = o[0:tq] - lam * o[tq:2 * tq]
    o_ref[...] = (_rmsnorm_rows(d, g_ref[...], SUBLN_EPS) * (1.0 - lam_init)).astype(BF16)


def _diff_attention(q, k, v, lq1, lk1, lq2, lk2, g, batch, seq, lam_init):
    tq = min(ROWS_ATTN // 2, seq)
    rows = 2 * tq
    nq = seq // tq
    vec = pl.BlockSpec((1, HEAD_DIM), lambda b, h, i: (0, 0))
    return pl.pallas_call(
        functools.partial(_diff_kernel, tq=tq, lam_init=lam_init),
        grid=(batch, DIFF_HEADS, nq),
        in_specs=[
            _SMEM_SPEC,
            pl.BlockSpec((tq, LANES), lambda b, h, i: (b * nq + i, h)),
            pl.BlockSpec((seq, LANES), lambda b, h, i: (b, h)),
            pl.BlockSpec((seq, LANES), lambda b, h, i: (b, h)),
            vec, vec, vec, vec,
            pl.BlockSpec((1, LANES), lambda b, h, i: (0, 0)),
        ],
        out_specs=pl.BlockSpec((tq, LANES), lambda b, h, i: (b * nq + i, h)),
        out_shape=jax.ShapeDtypeStruct(q.shape, BF16),
        scratch_shapes=_attn_scratch(rows, seq),
        compiler_params=_params("parallel", "parallel", "arbitrary"),
        name="diff_attention",
    )(_sweep_trips(rows), q, k, v, lq1, lk1, lq2, lk2, g)


def _na_band_start(n):
    return int(np.clip(n * NA_QCOLS - NA_WIN_COLS // 2, 0, GRID_W - NA_KCOLS))


def _na_bias_indices():
    qr = np.arange(NA_QROWS)[:, None]
    kr = np.arange(NA_KROWS)[None, :]
    half = NA_WIN_ROWS // 2
    key_shift = [0, -half, -(NA_KROWS - NA_QROWS)]
    win_start = [np.maximum(qr - half, 0), qr, np.minimum(qr + half, NA_KROWS - NA_WIN_ROWS)]
    drow = np.stack([np.clip(kr + shift - qr + NA_WIN_ROWS - 1, 0, 2 * NA_WIN_ROWS - 2) for shift in key_shift])
    rmask = np.stack([(kr >= w0) & (kr < w0 + NA_WIN_ROWS) for w0 in win_start])
    qc = np.arange(NA_QCOLS)[:, None]
    kc = np.arange(NA_KCOLS)[None, :]
    dcol, cmask = [], []
    for n in range(GRID_W // NA_QCOLS):
        qabs = n * NA_QCOLS + qc
        kabs = _na_band_start(n) + kc
        w0 = np.clip(qabs - NA_WIN_COLS // 2, 0, GRID_W - NA_WIN_COLS)
        dcol.append(np.clip(kabs - qabs + NA_WIN_COLS - 1, 0, 2 * NA_WIN_COLS - 2))
        cmask.append((kabs >= w0) & (kabs < w0 + NA_WIN_COLS))
    return drow, rmask, np.stack(dcol), np.stack(cmask)


def _na_bias_table(rpb):
    drow, rmask, dcol, cmask = _na_bias_indices()
    oh_r = jnp.asarray(np.eye(2 * NA_WIN_ROWS - 1, dtype=np.float32)[drow])
    oh_c = jnp.asarray(np.eye(2 * NA_WIN_COLS - 1, dtype=np.float32)[dcol])
    hp = lax.Precision.HIGHEST
    a = jnp.einsum("hab,vqka->hvqkb", rpb.astype(F32), oh_r, precision=hp)
    t = jnp.einsum("hvqkb,nxyb->vnhqxky", a, oh_c, precision=hp)
    mask = rmask[:, None, None, :, None, :, None] & cmask[None, :, None, None, :, None, :]
    t = jnp.where(jnp.asarray(mask), t * LOG2E, -jnp.inf)
    return t.reshape(t.shape[:3] + (NA_QROWS * NA_QCOLS, NA_KROWS * NA_KCOLS))


def _na_kernel(q_ref, k0, k1, k2, k3, v0, v1, v2, v3, tbl_ref, o_ref):
    kps = (k0, k1, k2, k3)
    vps = (v0, v1, v2, v3)
    lo = _lane_lo_mask()
    nq = NA_QROWS * NA_QCOLS
    for n in range(GRID_W // NA_QCOLS):
        band = _na_band_start(n)
        for c in range(A_WIDTH // LANES):
            lanes = slice(c * LANES, (c + 1) * LANES)
            qs = jnp.concatenate(
                [q_ref[r * GRID_W + n * NA_QCOLS:r * GRID_W + (n + 1) * NA_QCOLS, lanes] for r in range(NA_QROWS)],
                axis=0)
            zero = jnp.zeros_like(qs)
            qm = jnp.concatenate([jnp.where(lo, qs, zero), jnp.where(lo, zero, qs)], axis=0)

            def band_rows(pieces):
                return jnp.concatenate(
                    [p[r * GRID_W + band:r * GRID_W + band + NA_KCOLS, lanes]
                     for p in pieces for r in range(NA_KPIECE)], axis=0).astype(BF16)

            kb = band_rows(kps)
            vb = band_rows(vps)
            s = _dot_nt(qm, kb)
            s = s + jnp.concatenate([tbl_ref[0, n, 2 * c], tbl_ref[0, n, 2 * c + 1]], axis=0)
            m = jnp.max(s, axis=-1, keepdims=True)
            p = jnp.exp2(s - m)
            l = jnp.sum(p, axis=-1, keepdims=True)
            o = _dot(p.astype(BF16), vb) / l
            res = jnp.where(lo, o[0:nq], o[nq:2 * nq]).astype(BF16)
            for r in range(NA_QROWS):
                o_ref[r * GRID_W + n * NA_QCOLS:r * GRID_W + (n + 1) * NA_QCOLS, lanes] = (
                    res[r * NA_QCOLS:(r + 1) * NA_QCOLS])


def _na_attention(q, k, v, tbl, batch, seq):
    grid_rows = seq // GRID_W
    steps = grid_rows // NA_QROWS
    tq = NA_QROWS * GRID_W
    tp = NA_KPIECE * GRID_W
    pieces_per_batch = grid_rows // NA_KPIECE
    npieces = NA_KROWS // NA_KPIECE
    shift = (NA_WIN_ROWS // 2) // NA_KPIECE

    def piece_map(i):
        def index(b, t):
            start = jnp.clip(t * (NA_QROWS // NA_KPIECE) - shift, 0, pieces_per_batch - npieces)
            return (b * pieces_per_batch + start + i, 0)
        return index

    def tbl_map(b, t):
        variant = jnp.where(t == 0, 0, jnp.where(t == steps - 1, 2, 1))
        return (variant, 0, 0, 0, 0)

    piece_specs = [pl.BlockSpec((tp, A_WIDTH), piece_map(i)) for i in range(npieces)]
    return pl.pallas_call(
        _na_kernel,
        grid=(batch, steps),
        in_specs=[pl.BlockSpec((tq, A_WIDTH), lambda b, t: (b * steps + t, 0))] + piece_specs + piece_specs + [
            pl.BlockSpec((1,) + tbl.shape[1:], tbl_map)],
        out_specs=pl.BlockSpec((tq, A_WIDTH), lambda b, t: (b * steps + t, 0)),
        out_shape=jax.ShapeDtypeStruct(q.shape, BF16),
        compiler_params=_params("parallel", "arbitrary"),
        name="na_attention",
    )(q, k, k, k, k, v, v, v, v, tbl)


def _mlp_kernel(*refs, n_attn, final):
    x_ref = refs[0]
    attn_refs = refs[1:1 + n_attn]
    wo_ref, g_ref, wup_ref, wdown_ref, gf_ref, o_ref, h_ref, acc_ref = refs[1 + n_attn:]
    x = x_ref[...]
    width = wo_ref.shape[0] // n_attn
    for i, a_ref in enumerate(attn_refs):
        x = x + _dot(a_ref[...], wo_ref[i * width:(i + 1) * width, :])
    acc_ref[...] = x
    h_ref[...] = _rmsnorm_rows(x, g_ref[...], NORM_EPS).astype(BF16)

    def hidden_chunk(j, carry):
        u = jnp.maximum(_dot(h_ref[...], wup_ref[j]), 0.0)
        acc_ref[...] += _dot((u * u).astype(BF16), wdown_ref[j])
        return carry

    lax.fori_loop(0, wup_ref.shape[0], hidden_chunk, 0)
    y = acc_ref[...]
    if final:
        y = _rmsnorm_rows(y, gf_ref[...], NORM_EPS)
    o_ref[...] = y


def _outproj_mlp(x, attn, wo, g, wup, wdown, gf, final):
    n = x.shape[0]
    tm = min(TM_MLP, n)
    row = lambda i: (i, 0)
    const2 = lambda i: (0, 0)
    const3 = lambda i: (0, 0, 0)
    return pl.pallas_call(
        functools.partial(_mlp_kernel, n_attn=len(attn), final=final),
        grid=(n // tm,),
        in_specs=[pl.BlockSpec((tm, D_MODEL), row)]
        + [pl.BlockSpec((tm, a.shape[1]), row) for a in attn]
        + [
            pl.BlockSpec(wo.shape, const2),
            pl.BlockSpec((1, D_MODEL), const2),
            pl.BlockSpec(wup.shape, const3),
            pl.BlockSpec(wdown.shape, const3),
            pl.BlockSpec((1, D_MODEL), const2),
        ],
        out_specs=pl.BlockSpec((tm, D_MODEL), row),
        out_shape=jax.ShapeDtypeStruct(x.shape, F32),
        scratch_shapes=[pltpu.VMEM((tm, D_MODEL), BF16), pltpu.VMEM((tm, D_MODEL), F32)],
        compiler_params=_params("parallel"),
        name="outproj_mlp",
    )(x, *attn, wo, g, wup, wdown, gf)


def _rope_angles(pos, dim, theta):
    inv_freq = 1.0 / jnp.power(theta, jnp.arange(0, dim, 2, dtype=F32) / dim)
    ang = pos.astype(F32)[:, None] * inv_freq[None, :]
    return jnp.cos(ang), jnp.sin(ang)


def _axial_tables(seq):
    t = jnp.arange(seq)
    half = HEAD_DIM // 2
    cr, sr = _rope_angles(t // GRID_W, half, AXIAL_THETA)
    cc, sc = _rope_angles(t % GRID_W, half, AXIAL_THETA)
    cos = jnp.concatenate([cr, cr, cc, cc], axis=-1)
    sin = jnp.concatenate([-sr, sr, -sc, sc], axis=-1)
    return jnp.tile(cos, (1, LANES // HEAD_DIM)), jnp.tile(sin, (1, LANES // HEAD_DIM))


def _rope_tables(seq):
    c, s = _rope_angles(jnp.arange(seq), HEAD_DIM, ROPE_THETA)
    cos = jnp.concatenate([c, c], axis=-1)
    sin = jnp.concatenate([-s, s], axis=-1)
    return jnp.tile(cos, (1, LANES // HEAD_DIM)), jnp.tile(sin, (1, LANES // HEAD_DIM))


_GQA_HEAD_ORDER = np.arange(GQA_Q_HEADS).reshape(GQA_KV_HEADS, -1).T.reshape(-1)


def _prepare_even(w_in, q_norm, k_norm, w_out):
    a, bq = A_WIDTH, B_Q_WIDTH
    wqb = w_in[:, 3 * a:3 * a + bq].reshape(w_in.shape[0], GQA_Q_HEADS, HEAD_DIM)[:, _GQA_HEAD_ORDER, :]
    w = jnp.concatenate([w_in[:, :3 * a], wqb.reshape(w_in.shape[0], bq), w_in[:, 3 * a + bq:]], axis=1).astype(BF16)
    reps = LANES // HEAD_DIM
    gq = jnp.tile(q_norm, reps)[None]
    gk = jnp.tile(k_norm, reps)[None]
    wo_b = w_out[a:].reshape(GQA_Q_HEADS, HEAD_DIM, -1)[_GQA_HEAD_ORDER].reshape(bq, -1)
    wo = jnp.concatenate([w_out[:a], wo_b], axis=0).astype(BF16)
    return w, gq, gk, wo


def _group_ones():
    g = np.arange(LANES) // HEAD_DIM
    return jnp.asarray((g[:, None] == g[None, :]).astype(np.float32), dtype=BF16)


def _trunk(x3, p):
    batch, seq, _ = x3.shape
    assert seq % (NA_KROWS * GRID_W) == 0
    x = x3.reshape(batch * seq, D_MODEL)
    depth = len(p["layers"])
    for layer, lp in enumerate(p["layers"]):
        final = layer == depth - 1
        if layer % 2 == 0:
            cos, sin = _axial_tables(seq)
            qa, ka, va, qb, kb, vb = _inproj_even(x, lp["ln_mix"], lp["w_in"], p["gmat"], lp["gq"], lp["gk"],
                                                  cos, sin, seq)
            a_out = _na_attention(qa, ka, va, lp["na_bias"], batch, seq)
            b_out = _gqa_attention(qb, kb, vb, batch, seq)
            attn = [a_out, b_out]
        else:
            cos, sin = _rope_tables(seq)
            q, k, v = _inproj_odd(x, lp["ln_mix"], lp["w_in"], cos, sin, seq)
            attn = [_diff_attention(q, k, v, lp["lq1"], lp["lk1"], lp["lq2"], lp["lk2"], lp["subln"],
                                    batch, seq, lp["lam_init"])]
        x = _outproj_mlp(x, attn, lp["w_out"], lp["ln_mlp"], lp["w_up"], lp["w_down"], p["ln_f"], final)
    return x.reshape(batch, seq, D_MODEL)


def kernel(x_prompt, x_sample, ln_mix_e, w_in_e, rpb, q_norm_b, k_norm_b, w_out_e, ln_mix_o, w_in_o, lambda_q1, lambda_k1, lambda_q2, lambda_k2, subln_g, w_out_o, ln_mlp, w_up, w_down, ln_f):
    depth = ln_mlp.shape[0]
    layers = []
    for layer in range(depth):
        j = layer // 2
        lp = {
            "ln_mlp": ln_mlp[layer][None],
            "w_up": w_up[layer].astype(BF16).reshape(D_MODEL, D_FF // TF_MLP, TF_MLP).transpose(1, 0, 2),
            "w_down": w_down[layer].astype(BF16).reshape(D_FF // TF_MLP, TF_MLP, D_MODEL),
        }
        if layer % 2 == 0:
            w, gq, gk, wo = _prepare_even(w_in_e[j], q_norm_b[j], k_norm_b[j], w_out_e[j])
            lp.update(ln_mix=ln_mix_e[j][None], w_in=w, gq=gq, gk=gk, w_out=wo,
                      na_bias=_na_bias_table(rpb[j]))
        else:
            lp.update(ln_mix=ln_mix_o[j][None], w_in=w_in_o[j].astype(BF16), w_out=w_out_o[j].astype(BF16),
                      lq1=lambda_q1[j][None], lk1=lambda_k1[j][None], lq2=lambda_q2[j][None],
                      lk2=lambda_k2[j][None], subln=subln_g[j][None],
                      lam_init=0.8 - 0.6 * math.exp(-0.3 * layer))
        layers.append(lp)
    p = {"layers": layers, "gmat": _group_ones(), "ln_f": ln_f[None]}
    return (_trunk(x_prompt, p), _trunk(x_sample, p))
```

```python
import functools
import math

import numpy as np
import jax
import jax.numpy as jnp
from jax import lax
from jax.experimental import pallas as pl
from jax.experimental.pallas import tpu as pltpu

F32 = jnp.float32
BF16 = jnp.bfloat16

D_MODEL = 1024
HEAD_DIM = 64
GRID_W = 64
NA_HEADS = 8
NA_WIN_ROWS = 8
NA_WIN_COLS = 16
NA_QCOLS = 16
NA_KCOLS = 32
GQA_Q_HEADS = 8
GQA_KV_HEADS = 2
AXIAL_THETA = 10000.0
DIFF_HEADS = 8
D_FF = 4 * D_MODEL
ROPE_THETA = 10000.0
NORM_EPS = 1e-6
QK_NORM_EPS = 1e-6
SUBLN_EPS = 1e-5
A_WIDTH = NA_HEADS * HEAD_DIM
B_Q_WIDTH = GQA_Q_HEADS * HEAD_DIM
B_KV_WIDTH = GQA_KV_HEADS * HEAD_DIM
DIFF_WIDTH = 2 * DIFF_HEADS * HEAD_DIM
SM_SCALE = HEAD_DIM ** -0.5
LOG2E = math.log2(math.e)
Q_SCALE = SM_SCALE * LOG2E

LANES = 128
VMEM_LIMIT = 56 * 1024 * 1024

TM_PROJ = 1024
TM_MLP = 1024
TF_MLP = 1024
SCORE_SCRATCH_BYTES = 32 * 1024 * 1024
ROWS_PV = 1024
NA_QROWS = 8
NA_KROWS = 16
NA_KPIECE = 4


def _params(*sem):
    return pltpu.CompilerParams(dimension_semantics=sem, vmem_limit_bytes=VMEM_LIMIT)


def _rmsnorm_rows(x, g, eps):
    ms = jnp.mean(x * x, axis=-1, keepdims=True)
    return x * lax.rsqrt(ms + eps) * g


def _dot(a, b):
    return jnp.dot(a, b, preferred_element_type=F32)


def _dot_nt(a, b):
    return lax.dot_general(a, b, (((1,), (1,)), ((), ())), preferred_element_type=F32)


def _lane_lo_mask():
    return lax.broadcasted_iota(jnp.int32, (1, LANES), 1) < HEAD_DIM


def _swap_halves(x, span):
    first = lax.broadcasted_iota(jnp.int32, (1, LANES), 1) % (2 * span) < span
    return jnp.where(first, pltpu.roll(x, LANES - span, 1), pltpu.roll(x, span, 1))


def _group_sumsq(x, gmat_ref):
    sq = x * x
    hi = sq.astype(BF16)
    lo = (sq - hi.astype(F32)).astype(BF16)
    return _dot(hi, gmat_ref[...]) + _dot(lo, gmat_ref[...])


def _inproj_even_kernel(x_ref, g_ref, w_ref, gmat_ref, gq_ref, gk_ref, cos_ref, sin_ref,
                        qa_ref, ka_ref, va_ref, qb_ref, kb_ref, vb_ref):
    h = _rmsnorm_rows(x_ref[...], g_ref[...], NORM_EPS).astype(BF16)
    a, bq, bkv = A_WIDTH, B_Q_WIDTH, B_KV_WIDTH
    qa_ref[...] = (_dot(h, w_ref[:, 0:a]) * Q_SCALE).astype(BF16)
    ka_ref[...] = _dot(h, w_ref[:, a:2 * a])
    va_ref[...] = _dot(h, w_ref[:, 2 * a:3 * a])
    o = 3 * a
    q = _dot(h, w_ref[:, o:o + bq])
    cos = cos_ref[...]
    sin = sin_ref[...]

    def norm_rope(x, g):
        r = lax.rsqrt(_group_sumsq(x, gmat_ref) * (1.0 / HEAD_DIM) + QK_NORM_EPS)
        xg = x * g
        return r * (xg * cos + _swap_halves(xg, HEAD_DIM // 4) * sin)

    for c in range(bq // LANES):
        sl = slice(c * LANES, (c + 1) * LANES)
        qb_ref[:, sl] = (norm_rope(q[:, sl], gq_ref[...]) * Q_SCALE).astype(BF16)
    o += bq
    kb_ref[...] = norm_rope(_dot(h, w_ref[:, o:o + bkv]), gk_ref[...]).astype(BF16)
    o += bkv
    vb_ref[...] = _dot(h, w_ref[:, o:o + bkv]).astype(BF16)


def _inproj_even(x, g, w, gmat, gq, gk, cos, sin, seq):
    n = x.shape[0]
    tm = min(TM_PROJ, seq)
    pos_blocks = seq // tm
    row = lambda i: (i, 0)
    const = lambda i: (0, 0)
    pos = lambda i: (i % pos_blocks, 0)
    wcols = w.shape[1]
    return pl.pallas_call(
        _inproj_even_kernel,
        grid=(n // tm,),
        in_specs=[
            pl.BlockSpec((tm, D_MODEL), row),
            pl.BlockSpec((1, D_MODEL), const),
            pl.BlockSpec((D_MODEL, wcols), const),
            pl.BlockSpec((LANES, LANES), const),
            pl.BlockSpec((1, LANES), const),
            pl.BlockSpec((1, LANES), const),
            pl.BlockSpec((tm, LANES), pos),
            pl.BlockSpec((tm, LANES), pos),
        ],
        out_specs=[
            pl.BlockSpec((tm, A_WIDTH), row),
            pl.BlockSpec((tm, A_WIDTH), row),
            pl.BlockSpec((tm, A_WIDTH), row),
            pl.BlockSpec((tm, B_Q_WIDTH), row),
            pl.BlockSpec((tm, B_KV_WIDTH), row),
            pl.BlockSpec((tm, B_KV_WIDTH), row),
        ],
        out_shape=[
            jax.ShapeDtypeStruct((n, A_WIDTH), BF16),
            jax.ShapeDtypeStruct((n, A_WIDTH), F32),
            jax.ShapeDtypeStruct((n, A_WIDTH), F32),
            jax.ShapeDtypeStruct((n, B_Q_WIDTH), BF16),
            jax.ShapeDtypeStruct((n, B_KV_WIDTH), BF16),
            jax.ShapeDtypeStruct((n, B_KV_WIDTH), BF16),
        ],
        compiler_params=_params("parallel"),
        name="inproj_even",
    )(x, g, w, gmat, gq, gk, cos, sin)


def _inproj_odd_kernel(x_ref, g_ref, w_ref, cos_ref, sin_ref, q_ref, k_ref, v_ref):
    h = _rmsnorm_rows(x_ref[...], g_ref[...], NORM_EPS).astype(BF16)
    d = DIFF_WIDTH
    cos = cos_ref[...]
    sin = sin_ref[...]

    def rope(o, scale, out_ref):
        x = _dot(h, w_ref[:, o:o + d])
        for c in range(d // LANES):
            xc = x[:, c * LANES:(c + 1) * LANES]
            out = xc * cos + _swap_halves(xc, HEAD_DIM // 2) * sin
            out_ref[:, c * LANES:(c + 1) * LANES] = (out * scale).astype(BF16)

    rope(0, Q_SCALE, q_ref)
    rope(d, 1.0, k_ref)
    v_ref[...] = _dot(h, w_ref[:, 2 * d:3 * d]).astype(BF16)


def _inproj_odd(x, g, w, cos, sin, seq):
    n = x.shape[0]
    tm = min(TM_PROJ, seq)
    pos_blocks = seq // tm
    row = lambda i: (i, 0)
    const = lambda i: (0, 0)
    pos = lambda i: (i % pos_blocks, 0)
    out = jax.ShapeDtypeStruct((n, DIFF_WIDTH), BF16)
    return pl.pallas_call(
        _inproj_odd_kernel,
        grid=(n // tm,),
        in_specs=[
            pl.BlockSpec((tm, D_MODEL), row),
            pl.BlockSpec((1, D_MODEL), const),
            pl.BlockSpec((D_MODEL, w.shape[1]), const),
            pl.BlockSpec((tm, LANES), pos),
            pl.BlockSpec((tm, LANES), pos),
        ],
        out_specs=[pl.BlockSpec((tm, DIFF_WIDTH), row)] * 3,
        out_shape=[out, out, out],
        compiler_params=_params("parallel"),
        name="inproj_odd",
    )(x, g, w, cos, sin)


def _stack_masked_heads(q_ref, qm_ref, ncols, tq):
    lo = _lane_lo_mask()
    for c in range(ncols):
        qc = q_ref[:, c * LANES:(c + 1) * LANES]
        zero = jnp.zeros_like(qc)
        qm_ref[2 * c * tq:(2 * c + 1) * tq, :] = jnp.where(lo, qc, zero)
        qm_ref[(2 * c + 1) * tq:(2 * c + 2) * tq, :] = jnp.where(lo, zero, qc)


def _softmax_pv(first_tile, trips_ref, qm_ref, k_ref, v_ref, s_ref, m_ref, vext_ref, acc_ref):
    seq = k_ref.shape[0]
    rb = min(ROWS_PV, qm_ref.shape[0])

    @pl.when(first_tile)
    def _():
        vext_ref[:, 0:LANES] = v_ref[...]
        vext_ref[:, LANES:2 * LANES] = jnp.ones(v_ref.shape, BF16)

    def scores(j, carry):
        s = _dot_nt(qm_ref[...], k_ref[...])
        s_ref[...] = s
        m = s[:, 0:LANES]
        for i in range(1, seq // LANES):
            m = jnp.maximum(m, s[:, i * LANES:(i + 1) * LANES])
        m_ref[...] = jnp.broadcast_to(jnp.max(m, axis=-1, keepdims=True), m_ref.shape)
        return carry

    lax.fori_loop(0, trips_ref[0], scores, 0)

    def values(r, carry):
        rows = pl.ds(pl.multiple_of(r * rb, rb), rb)
        s = s_ref[rows, :]
        m = m_ref[rows, :]
        ps = [jnp.exp2(s[:, i * LANES:(i + 1) * LANES] - m).astype(BF16) for i in range(seq // LANES)]
        acc_ref[rows, :] = _dot(jnp.concatenate(ps, axis=1), vext_ref[...])
        return carry

    lax.fori_loop(0, trips_ref[1], values, 0)


_SMEM_SPEC = pl.BlockSpec(memory_space=pltpu.SMEM)


def _attn_rows(seq):
    return SCORE_SCRATCH_BYTES // (4 * seq)


def _sweep_trips(rows):
    return jnp.asarray([1, rows // min(ROWS_PV, rows)], jnp.int32)


def _attn_scratch(rows, seq):
    return [
        pltpu.VMEM((rows, LANES), BF16),
        pltpu.VMEM((rows, seq), F32),
        pltpu.VMEM((rows, LANES), F32),
        pltpu.VMEM((seq, 2 * LANES), BF16),
        pltpu.VMEM((rows, 2 * LANES), F32),
    ]


def _gqa_kernel(trips_ref, q_ref, k_ref, v_ref, o_ref, qm_ref, s_ref, m_ref, vext_ref, acc_ref, *, tq):
    ncols = B_Q_WIDTH // LANES
    _stack_masked_heads(q_ref, qm_ref, ncols, tq)
    _softmax_pv(pl.program_id(1) == 0, trips_ref, qm_ref, k_ref, v_ref, s_ref, m_ref, vext_ref, acc_ref)
    lo = _lane_lo_mask()
    for c in range(ncols):
        a_lo = acc_ref[2 * c * tq:(2 * c + 1) * tq, :]
        a_hi = acc_ref[(2 * c + 1) * tq:(2 * c + 2) * tq, :]
        num = jnp.where(lo, a_lo[:, 0:LANES], a_hi[:, 0:LANES])
        den = jnp.where(lo, a_lo[:, LANES:2 * LANES], a_hi[:, LANES:2 * LANES])
        o_ref[:, c * LANES:(c + 1) * LANES] = (num / den).astype(BF16)


def _gqa_attention(q, k, v, batch, seq):
    ncols = B_Q_WIDTH // LANES
    tq = min(_attn_rows(seq) // (2 * ncols), seq)
    rows = 2 * ncols * tq
    nq = seq // tq
    return pl.pallas_call(
        functools.partial(_gqa_kernel, tq=tq),
        grid=(batch, nq),
        in_specs=[
            _SMEM_SPEC,
            pl.BlockSpec((tq, B_Q_WIDTH), lambda b, i: (b * nq + i, 0)),
            pl.BlockSpec((seq, B_KV_WIDTH), lambda b, i: (b, 0)),
            pl.BlockSpec((seq, B_KV_WIDTH), lambda b, i: (b, 0)),
        ],
        out_specs=pl.BlockSpec((tq, B_Q_WIDTH), lambda b, i: (b * nq + i, 0)),
        out_shape=jax.ShapeDtypeStruct(q.shape, BF16),
        scratch_shapes=_attn_scratch(rows, seq),
        compiler_params=_params("parallel", "arbitrary"),
        name="gqa_attention",
    )(_sweep_trips(rows), q, k, v)


def _diff_kernel(trips_ref, q_ref, k_ref, v_ref, lq1_ref, lk1_ref, lq2_ref, lk2_ref, g_ref, o_ref,
                 qm_ref, s_ref, m_ref, vext_ref, acc_ref, *, tq, lam_init):
    _stack_masked_heads(q_ref, qm_ref, 1, tq)
    _softmax_pv(pl.program_id(2) == 0, trips_ref, qm_ref, k_ref, v_ref, s_ref, m_ref, vext_ref, acc_ref)
    o = acc_ref[:, 0:LANES] / acc_ref[:, LANES:2 * LANES]
    lam = (jnp.exp(jnp.sum(lq1_ref[...] * lk1_ref[...], axis=-1, keepdims=True))
           - jnp.exp(jnp.sum(lq2_ref[...] * lk2_ref[...], axis=-1, keepdims=True)) + lam_init)
    d = o[0:tq] - lam * o[tq:2 * tq]
    o_ref[...] = (_rmsnorm_rows(d, g_ref[...], SUBLN_EPS) * (1.0 - lam_init)).astype(BF16)


def _diff_attention(q, k, v, lq1, lk1, lq2, lk2, g, batch, seq, lam_init):
    tq = min(_attn_rows(seq) // 2, seq)
    rows = 2 * tq
    nq = seq // tq
    vec = pl.BlockSpec((1, HEAD_DIM), lambda b, h, i: (0, 0))
    return pl.pallas_call(
        functools.partial(_diff_kernel, tq=tq, lam_init=lam_init),
        grid=(batch, DIFF_HEADS, nq),
        in_specs=[
            _SMEM_SPEC,
            pl.BlockSpec((tq, LANES), lambda b, h, i: (b * nq + i, h)),
            pl.BlockSpec((seq, LANES), lambda b, h, i: (b, h)),
            pl.BlockSpec((seq, LANES), lambda b, h, i: (b, h)),
            vec, vec, vec, vec,
            pl.BlockSpec((1, LANES), lambda b, h, i: (0, 0)),
        ],
        out_specs=pl.BlockSpec((tq, LANES), lambda b, h, i: (b * nq + i, h)),
        out_shape=jax.ShapeDtypeStruct(q.shape, BF16),
        scratch_shapes=_attn_scratch(rows, seq),
        compiler_params=_params("parallel", "parallel", "arbitrary"),
        name="diff_attention",
    )(_sweep_trips(rows), q, k, v, lq1, lk1, lq2, lk2, g)


def _na_band_start(n):
    return int(np.clip(n * NA_QCOLS - NA_WIN_COLS // 2, 0, GRID_W - NA_KCOLS))


def _na_bias_indices():
    qr = np.arange(NA_QROWS)[:, None]
    kr = np.arange(NA_KROWS)[None, :]
    half = NA_WIN_ROWS // 2
    key_shift = [0, -half, -(NA_KROWS - NA_QROWS)]
    win_start = [np.maximum(qr - half, 0), qr, np.minimum(qr + half, NA_KROWS - NA_WIN_ROWS)]
    drow = np.stack([np.clip(kr + shift - qr + NA_WIN_ROWS - 1, 0, 2 * NA_WIN_ROWS - 2) for shift in key_shift])
    rmask = np.stack([(kr >= w0) & (kr < w0 + NA_WIN_ROWS) for w0 in win_start])
    qc = np.arange(NA_QCOLS)[:, None]
    kc = np.arange(NA_KCOLS)[None, :]
    dcol, cmask = [], []
    for n in range(GRID_W // NA_QCOLS):
        qabs = n * NA_QCOLS + qc
        kabs = _na_band_start(n) + kc
        w0 = np.clip(qabs - NA_WIN_COLS // 2, 0, GRID_W - NA_WIN_COLS)
        dcol.append(np.clip(kabs - qabs + NA_WIN_COLS - 1, 0, 2 * NA_WIN_COLS - 2))
        cmask.append((kabs >= w0) & (kabs < w0 + NA_WIN_COLS))
    return drow, rmask, np.stack(dcol), np.stack(cmask)


def _na_bias_table(rpb):
    drow, rmask, dcol, cmask = _na_bias_indices()
    oh_r = jnp.asarray(np.eye(2 * NA_WIN_ROWS - 1, dtype=np.float32)[drow])
    oh_c = jnp.asarray(np.eye(2 * NA_WIN_COLS - 1, dtype=np.float32)[dcol])
    hp = lax.Precision.HIGHEST
    a = jnp.einsum("hab,vqka->hvqkb", rpb.astype(F32), oh_r, precision=hp)
    t = jnp.einsum("hvqkb,nxyb->vnhqxky", a, oh_c, precision=hp)
    mask = rmask[:, None, None, :, None, :, None] & cmask[None, :, None, None, :, None, :]
    t = jnp.where(jnp.asarray(mask), t * LOG2E, -jnp.inf)
    return t.reshape(t.shape[:3] + (NA_QROWS * NA_QCOLS, NA_KROWS * NA_KCOLS))


def _na_kernel(q_ref, k0, k1, k2, k3, v0, v1, v2, v3, tbl_ref, o_ref):
    kps = (k0, k1, k2, k3)
    vps = (v0, v1, v2, v3)
    lo = _lane_lo_mask()
    nq = NA_QROWS * NA_QCOLS
    for n in range(GRID_W // NA_QCOLS):
        band = _na_band_start(n)
        for c in range(A_WIDTH // LANES):
            lanes = slice(c * LANES, (c + 1) * LANES)
            qs = jnp.concatenate(
                [q_ref[r * GRID_W + n * NA_QCOLS:r * GRID_W + (n + 1) * NA_QCOLS, lanes] for r in range(NA_QROWS)],
                axis=0)
            zero = jnp.zeros_like(qs)
            qm = jnp.concatenate([jnp.where(lo, qs, zero), jnp.where(lo, zero, qs)], axis=0)

            def band_rows(pieces):
                return jnp.concatenate(
                    [p[r * GRID_W + band:r * GRID_W + band + NA_KCOLS, lanes]
                     for p in pieces for r in range(NA_KPIECE)], axis=0).astype(BF16)

            kb = band_rows(kps)
            vb = band_rows(vps)
            s = _dot_nt(qm, kb)
            s = s + jnp.concatenate([tbl_ref[0, n, 2 * c], tbl_ref[0, n, 2 * c + 1]], axis=0)
            m = jnp.max(s, axis=-1, keepdims=True)
            p = jnp.exp2(s - m)
            l = jnp.sum(p, axis=-1, keepdims=True)
            o = _dot(p.astype(BF16), vb) / l
            res = jnp.where(lo, o[0:nq], o[nq:2 * nq]).astype(BF16)
            for r in range(NA_QROWS):
                o_ref[r * GRID_W + n * NA_QCOLS:r * GRID_W + (n + 1) * NA_QCOLS, lanes] = (
                    res[r * NA_QCOLS:(r + 1) * NA_QCOLS])


def _na_attention(q, k, v, tbl, batch, seq):
    grid_rows = seq // GRID_W
    steps = grid_rows // NA_QROWS
    tq = NA_QROWS * GRID_W
    tp = NA_KPIECE * GRID_W
    pieces_per_batch = grid_rows // NA_KPIECE
    npieces = NA_KROWS // NA_KPIECE
    shift = (NA_WIN_ROWS // 2) // NA_KPIECE

    def piece_map(i):
        def index(b, t):
            start = jnp.clip(t * (NA_QROWS // NA_KPIECE) - shift, 0, pieces_per_batch - npieces)
            return (b * pieces_per_batch + start + i, 0)
        return index

    def tbl_map(b, t):
        variant = jnp.where(t == 0, 0, jnp.where(t == steps - 1, 2, 1))
        return (variant, 0, 0, 0, 0)

    piece_specs = [pl.BlockSpec((tp, A_WIDTH), piece_map(i)) for i in range(npieces)]
    return pl.pallas_call(
        _na_kernel,
        grid=(batch, steps),
        in_specs=[pl.BlockSpec((tq, A_WIDTH), lambda b, t: (b * steps + t, 0))] + piece_specs + piece_specs + [
            pl.BlockSpec((1,) + tbl.shape[1:], tbl_map)],
        out_specs=pl.BlockSpec((tq, A_WIDTH), lambda b, t: (b * steps + t, 0)),
        out_shape=jax.ShapeDtypeStruct(q.shape, BF16),
        compiler_params=_params("parallel", "arbitrary"),
        name="na_attention",
    )(q, k, k, k, k, v, v, v, v, tbl)


def _mlp_kernel(*refs, n_attn, final):
    x_ref = refs[0]
    attn_refs = refs[1:1 + n_attn]
    wo_ref, g_ref, wup_ref, wdown_ref, gf_ref, o_ref, h_ref, acc_ref = refs[1 + n_attn:]
    x = x_ref[...]
    width = wo_ref.shape[0] // n_attn
    for i, a_ref in enumerate(attn_refs):
        x = x + _dot(a_ref[...], wo_ref[i * width:(i + 1) * width, :])
    acc_ref[...] = x
    h_ref[...] = _rmsnorm_rows(x, g_ref[...], NORM_EPS).astype(BF16)

    def hidden_chunk(j, carry):
        u = jnp.maximum(_dot(h_ref[...], wup_ref[j]), 0.0)
        acc_ref[...] += _dot((u * u).astype(BF16), wdown_ref[j])
        return carry

    lax.fori_loop(0, wup_ref.shape[0], hidden_chunk, 0)
    y = acc_ref[...]
    if final:
        y = _rmsnorm_rows(y, gf_ref[...], NORM_EPS)
    o_ref[...] = y


def _outproj_mlp(x, attn, wo, g, wup, wdown, gf, final):
    n = x.shape[0]
    tm = min(TM_MLP, n)
    row = lambda i: (i, 0)
    const2 = lambda i: (0, 0)
    const3 = lambda i: (0, 0, 0)
    return pl.pallas_call(
        functools.partial(_mlp_kernel, n_attn=len(attn), final=final),
        grid=(n // tm,),
        in_specs=[pl.BlockSpec((tm, D_MODEL), row)]
        + [pl.BlockSpec((tm, a.shape[1]), row) for a in attn]
        + [
            pl.BlockSpec(wo.shape, const2),
            pl.BlockSpec((1, D_MODEL), const2),
            pl.BlockSpec(wup.shape, const3),
            pl.BlockSpec(wdown.shape, const3),
            pl.BlockSpec((1, D_MODEL), const2),
        ],
        out_specs=pl.BlockSpec((tm, D_MODEL), row),
        out_shape=jax.ShapeDtypeStruct(x.shape, F32),
        scratch_shapes=[pltpu.VMEM((tm, D_MODEL), BF16), pltpu.VMEM((tm, D_MODEL), F32)],
        compiler_params=_params("parallel"),
        name="outproj_mlp",
    )(x, *attn, wo, g, wup, wdown, gf)


def _rope_angles(pos, dim, theta):
    inv_freq = 1.0 / jnp.power(theta, jnp.arange(0, dim, 2, dtype=F32) / dim)
    ang = pos.astype(F32)[:, None] * inv_freq[None, :]
    return jnp.cos(ang), jnp.sin(ang)


def _axial_tables(seq):
    t = jnp.arange(seq)
    half = HEAD_DIM // 2
    cr, sr = _rope_angles(t // GRID_W, half, AXIAL_THETA)
    cc, sc = _rope_angles(t % GRID_W, half, AXIAL_THETA)
    cos = jnp.concatenate([cr, cr, cc, cc], axis=-1)
    sin = jnp.concatenate([-sr, sr, -sc, sc], axis=-1)
    return jnp.tile(cos, (1, LANES // HEAD_DIM)), jnp.tile(sin, (1, LANES // HEAD_DIM))


def _rope_tables(seq):
    c, s = _rope_angles(jnp.arange(seq), HEAD_DIM, ROPE_THETA)
    cos = jnp.concatenate([c, c], axis=-1)
    sin = jnp.concatenate([-s, s], axis=-1)
    return jnp.tile(cos, (1, LANES // HEAD_DIM)), jnp.tile(sin, (1, LANES // HEAD_DIM))


_GQA_HEAD_ORDER = np.arange(GQA_Q_HEADS).reshape(GQA_KV_HEADS, -1).T.reshape(-1)


def _prepare_even(w_in, q_norm, k_norm, w_out):
    a, bq = A_WIDTH, B_Q_WIDTH
    wqb = w_in[:, 3 * a:3 * a + bq].reshape(w_in.shape[0], GQA_Q_HEADS, HEAD_DIM)[:, _GQA_HEAD_ORDER, :]
    w = jnp.concatenate([w_in[:, :3 * a], wqb.reshape(w_in.shape[0], bq), w_in[:, 3 * a + bq:]], axis=1).astype(BF16)
    reps = LANES // HEAD_DIM
    gq = jnp.tile(q_norm, reps)[None]
    gk = jnp.tile(k_norm, reps)[None]
    wo_b = w_out[a:].reshape(GQA_Q_HEADS, HEAD_DIM, -1)[_GQA_HEAD_ORDER].reshape(bq, -1)
    wo = jnp.concatenate([w_out[:a], wo_b], axis=0).astype(BF16)
    return w, gq, gk, wo


def _group_ones():
    g = np.arange(LANES) // HEAD_DIM
    return jnp.asarray((g[:, None] == g[None, :]).astype(np.float32), dtype=BF16)


def _trunk(x3, p):
    batch, seq, _ = x3.shape
    assert seq % (NA_KROWS * GRID_W) == 0
    x = x3.reshape(batch * seq, D_MODEL)
    depth = len(p["layers"])
    for layer, lp in enumerate(p["layers"]):
        final = layer == depth - 1
        if layer % 2 == 0:
            cos, sin = _axial_tables(seq)
            qa, ka, va, qb, kb, vb = _inproj_even(x, lp["ln_mix"], lp["w_in"], p["gmat"], lp["gq"], lp["gk"],
                                                  cos, sin, seq)
            a_out = _na_attention(qa, ka, va, lp["na_bias"], batch, seq)
            b_out = _gqa_attention(qb, kb, vb, batch, seq)
            attn = [a_out, b_out]
        else:
            cos, sin = _rope_tables(seq)
            q, k, v = _inproj_odd(x, lp["ln_mix"], lp["w_in"], cos, sin, seq)
            attn = [_diff_attention(q, k, v, lp["lq1"], lp["lk1"], lp["lq2"], lp["lk2"], lp["subln"],
                                    batch, seq, lp["lam_init"])]
        x = _outproj_mlp(x, attn, lp["w_out"], lp["ln_mlp"], lp["w_up"], lp["w_down"], p["ln_f"], final)
    return x.reshape(batch, seq, D_MODEL)


def kernel(x_prompt, x_sample, ln_mix_e, w_in_e, rpb, q_norm_b, k_norm_b, w_out_e, ln_mix_o, w_in_o, lambda_q1, lambda_k1, lambda_q2, lambda_k2, subln_g, w_out_o, ln_mlp, w_up, w_down, ln_f):
    depth = ln_mlp.shape[0]
    layers = []
    for layer in range(depth):
        j = layer // 2
        lp = {
            "ln_mlp": ln_mlp[layer][None],
            "w_up": w_up[layer].astype(BF16).reshape(D_MODEL, D_FF // TF_MLP, TF_MLP).transpose(1, 0, 2),
            "w_down": w_down[layer].astype(BF16).reshape(D_FF // TF_MLP, TF_MLP, D_MODEL),
        }
        if layer % 2 == 0:
            w, gq, gk, wo = _prepare_even(w_in_e[j], q_norm_b[j], k_norm_b[j], w_out_e[j])
            lp.update(ln_mix=ln_mix_e[j][None], w_in=w, gq=gq, gk=gk, w_out=wo,
                      na_bias=_na_bias_table(rpb[j]))
        else:
            lp.update(ln_mix=ln_mix_o[j][None], w_in=w_in_o[j].astype(BF16), w_out=w_out_o[j].astype(BF16),
                      lq1=lambda_q1[j][None], lk1=lambda_k1[j][None], lq2=lambda_q2[j][None],
                      lk2=lambda_k2[j][None], subln=subln_g[j][None],
                      lam_init=0.8 - 0.6 * math.exp(-0.3 * layer))
        layers.append(lp)
    p = {"layers": layers, "gmat": _group_ones(), "ln_f": ln_f[None]}
    return (_trunk(x_prompt, p), _trunk(x_sample, p))
```

```python
import functools
import math

import numpy as np
import jax
import jax.numpy as jnp
from jax import lax
from jax.experimental import pallas as pl
from jax.experimental.pallas import tpu as pltpu

F32 = jnp.float32
BF16 = jnp.bfloat16

D_MODEL = 1024
HEAD_DIM = 64
GRID_W = 64
NA_HEADS = 8
NA_WIN_ROWS = 8
NA_WIN_COLS = 16
NA_QCOLS = 16
NA_KCOLS = 32
GQA_Q_HEADS = 8
GQA_KV_HEADS = 2
AXIAL_THETA = 10000.0
DIFF_HEADS = 8
D_FF = 4 * D_MODEL
ROPE_THETA = 10000.0
NORM_EPS = 1e-6
QK_NORM_EPS = 1e-6
SUBLN_EPS = 1e-5
A_WIDTH = NA_HEADS * HEAD_DIM
B_Q_WIDTH = GQA_Q_HEADS * HEAD_DIM
B_KV_WIDTH = GQA_KV_HEADS * HEAD_DIM
DIFF_WIDTH = 2 * DIFF_HEADS * HEAD_DIM
SM_SCALE = HEAD_DIM ** -0.5
LOG2E = math.log2(math.e)
Q_SCALE = SM_SCALE * LOG2E

LANES = 128
VMEM_LIMIT = 56 * 1024 * 1024

TM_PROJ = 1024
TM_MLP = 1024
TF_MLP = 1024
SCORE_SCRATCH_BYTES = 32 * 1024 * 1024
ROWS_PV = 1024
NA_QROWS = 8
NA_KROWS = 16
NA_KPIECE = 4


def _params(*sem):
    return pltpu.CompilerParams(dimension_semantics=sem, vmem_limit_bytes=VMEM_LIMIT)


def _rmsnorm_rows(x, g, eps):
    ms = jnp.mean(x * x, axis=-1, keepdims=True)
    return x * lax.rsqrt(ms + eps) * g


def _dot(a, b):
    return jnp.dot(a, b, preferred_element_type=F32)


def _dot_nt(a, b):
    return lax.dot_general(a, b, (((1,), (1,)), ((), ())), preferred_element_type=F32)


def _lane_lo_mask():
    return lax.broadcasted_iota(jnp.int32, (1, LANES), 1) < HEAD_DIM


def _swap_halves(x, span):
    first = lax.broadcasted_iota(jnp.int32, (1, LANES), 1) % (2 * span) < span
    return jnp.where(first, pltpu.roll(x, LANES - span, 1), pltpu.roll(x, span, 1))


def _group_sumsq(x, gmat_ref):
    sq = x * x
    hi = sq.astype(BF16)
    lo = (sq - hi.astype(F32)).astype(BF16)
    return _dot(hi, gmat_ref[...]) + _dot(lo, gmat_ref[...])


def _inproj_even_kernel(x_ref, g_ref, w_ref, gmat_ref, gq_ref, gk_ref, cos_ref, sin_ref,
                        qa_ref, ka_ref, va_ref, qb_ref, kb_ref, vb_ref):
    h = _rmsnorm_rows(x_ref[...], g_ref[...], NORM_EPS).astype(BF16)
    a, bq, bkv = A_WIDTH, B_Q_WIDTH, B_KV_WIDTH
    qa_ref[...] = (_dot(h, w_ref[:, 0:a]) * Q_SCALE).astype(BF16)
    ka_ref[...] = _dot(h, w_ref[:, a:2 * a])
    va_ref[...] = _dot(h, w_ref[:, 2 * a:3 * a])
    o = 3 * a
    q = _dot(h, w_ref[:, o:o + bq])
    cos = cos_ref[...]
    sin = sin_ref[...]

    def norm_rope(x, g):
        r = lax.rsqrt(_group_sumsq(x, gmat_ref) * (1.0 / HEAD_DIM) + QK_NORM_EPS)
        xg = x * g
        return r * (xg * cos + _swap_halves(xg, HEAD_DIM // 4) * sin)

    for c in range(bq // LANES):
        sl = slice(c * LANES, (c + 1) * LANES)
        qb_ref[:, sl] = (norm_rope(q[:, sl], gq_ref[...]) * Q_SCALE).astype(BF16)
    o += bq
    kb_ref[...] = norm_rope(_dot(h, w_ref[:, o:o + bkv]), gk_ref[...]).astype(BF16)
    o += bkv
    vb_ref[...] = _dot(h, w_ref[:, o:o + bkv]).astype(BF16)


def _inproj_even(x, g, w, gmat, gq, gk, cos, sin, seq):
    n = x.shape[0]
    tm = min(TM_PROJ, seq)
    pos_blocks = seq // tm
    row = lambda i: (i, 0)
    const = lambda i: (0, 0)
    pos = lambda i: (i % pos_blocks, 0)
    wcols = w.shape[1]
    return pl.pallas_call(
        _inproj_even_kernel,
        grid=(n // tm,),
        in_specs=[
            pl.BlockSpec((tm, D_MODEL), row),
            pl.BlockSpec((1, D_MODEL), const),
            pl.BlockSpec((D_MODEL, wcols), const),
            pl.BlockSpec((LANES, LANES), const),
            pl.BlockSpec((1, LANES), const),
            pl.BlockSpec((1, LANES), const),
            pl.BlockSpec((tm, LANES), pos),
            pl.BlockSpec((tm, LANES), pos),
        ],
        out_specs=[
            pl.BlockSpec((tm, A_WIDTH), row),
            pl.BlockSpec((tm, A_WIDTH), row),
            pl.BlockSpec((tm, A_WIDTH), row),
            pl.BlockSpec((tm, B_Q_WIDTH), row),
            pl.BlockSpec((tm, B_KV_WIDTH), row),
            pl.BlockSpec((tm, B_KV_WIDTH), row),
        ],
        out_shape=[
            jax.ShapeDtypeStruct((n, A_WIDTH), BF16),
            jax.ShapeDtypeStruct((n, A_WIDTH), F32),
            jax.ShapeDtypeStruct((n, A_WIDTH), F32),
            jax.ShapeDtypeStruct((n, B_Q_WIDTH), BF16),
            jax.ShapeDtypeStruct((n, B_KV_WIDTH), BF16),
            jax.ShapeDtypeStruct((n, B_KV_WIDTH), BF16),
        ],
        compiler_params=_params("parallel"),
        name="inproj_even",
    )(x, g, w, gmat, gq, gk, cos, sin)


def _inproj_odd_kernel(x_ref, g_ref, w_ref, cos_ref, sin_ref, q_ref, k_ref, v_ref):
    h = _rmsnorm_rows(x_ref[...], g_ref[...], NORM_EPS).astype(BF16)
    d = DIFF_WIDTH
    cos = cos_ref[...]
    sin = sin_ref[...]

    def rope(o, scale, out_ref):
        x = _dot(h, w_ref[:, o:o + d])
        for c in range(d // LANES):
            xc = x[:, c * LANES:(c + 1) * LANES]
            out = xc * cos + _swap_halves(xc, HEAD_DIM // 2) * sin
            out_ref[:, c * LANES:(c + 1) * LANES] = (out * scale).astype(BF16)

    rope(0, Q_SCALE, q_ref)
    rope(d, 1.0, k_ref)
    v_ref[...] = _dot(h, w_ref[:, 2 * d:3 * d]).astype(BF16)


def _inproj_odd(x, g, w, cos, sin, seq):
    n = x.shape[0]
    tm = min(TM_PROJ, seq)
    pos_blocks = seq // tm
    row = lambda i: (i, 0)
    const = lambda i: (0, 0)
    pos = lambda i: (i % pos_blocks, 0)
    out = jax.ShapeDtypeStruct((n, DIFF_WIDTH), BF16)
    return pl.pallas_call(
        _inproj_odd_kernel,
        grid=(n // tm,),
        in_specs=[
            pl.BlockSpec((tm, D_MODEL), row),
            pl.BlockSpec((1, D_MODEL), const),
            pl.BlockSpec((D_MODEL, w.shape[1]), const),
            pl.BlockSpec((tm, LANES), pos),
            pl.BlockSpec((tm, LANES), pos),
        ],
        out_specs=[pl.BlockSpec((tm, DIFF_WIDTH), row)] * 3,
        out_shape=[out, out, out],
        compiler_params=_params("parallel"),
        name="inproj_odd",
    )(x, g, w, cos, sin)


def _stack_masked_heads(q_ref, qm_ref, ncols, tq):
    lo = _lane_lo_mask()
    for c in range(ncols):
        qc = q_ref[:, c * LANES:(c + 1) * LANES]
        zero = jnp.zeros_like(qc)
        qm_ref[2 * c * tq:(2 * c + 1) * tq, :] = jnp.where(lo, qc, zero)
        qm_ref[(2 * c + 1) * tq:(2 * c + 2) * tq, :] = jnp.where(lo, zero, qc)


def _softmax_pv(first_tile, trips_ref, qm_ref, k_ref, v_ref, s_ref, m_ref, vext_ref, acc_ref):
    seq = k_ref.shape[0]
    rb = min(ROWS_PV, qm_ref.shape[0])

    @pl.when(first_tile)
    def _():
        vext_ref[:, 0:LANES] = v_ref[...]
        vext_ref[:, LANES:2 * LANES] = jnp.ones(v_ref.shape, BF16)

    def scores(j, carry):
        s = _dot_nt(qm_ref[...], k_ref[...])
        s_ref[...] = s
        m = s[:, 0:LANES]
        for i in range(1, seq // LANES):
            m = jnp.maximum(m, s[:, i * LANES:(i + 1) * LANES])
        m_ref[...] = jnp.broadcast_to(jnp.max(m, axis=-1, keepdims=True), m_ref.shape)
        return carry

    lax.fori_loop(0, trips_ref[0], scores, 0)

    def values(r, carry):
        rows = pl.ds(pl.multiple_of(r * rb, rb), rb)
        s = s_ref[rows, :]
        m = m_ref[rows, :]
        ps = [jnp.exp2(s[:, i * LANES:(i + 1) * LANES] - m).astype(BF16) for i in range(seq // LANES)]
        acc_ref[rows, :] = _dot(jnp.concatenate(ps, axis=1), vext_ref[...])
        return carry

    lax.fori_loop(0, trips_ref[1], values, 0)


_SMEM_SPEC = pl.BlockSpec(memory_space=pltpu.SMEM)


def _attn_rows(seq):
    return SCORE_SCRATCH_BYTES // (4 * seq)


def _sweep_trips(rows):
    return jnp.asarray([1, rows // min(ROWS_PV, rows)], jnp.int32)


def _attn_scratch(rows, seq):
    return [
        pltpu.VMEM((rows, LANES), BF16),
        pltpu.VMEM((rows, seq), F32),
        pltpu.VMEM((rows, LANES), F32),
        pltpu.VMEM((seq, 2 * LANES), BF16),
        pltpu.VMEM((rows, 2 * LANES), F32),
    ]


def _gqa_kernel(trips_ref, q_ref, k_ref, v_ref, o_ref, qm_ref, s_ref, m_ref, vext_ref, acc_ref, *, tq):
    ncols = B_Q_WIDTH // LANES
    _stack_masked_heads(q_ref, qm_ref, ncols, tq)
    _softmax_pv(pl.program_id(1) == 0, trips_ref, qm_ref, k_ref, v_ref, s_ref, m_ref, vext_ref, acc_ref)
    lo = _lane_lo_mask()
    for c in range(ncols):
        a_lo = acc_ref[2 * c * tq:(2 * c + 1) * tq, :]
        a_hi = acc_ref[(2 * c + 1) * tq:(2 * c + 2) * tq, :]
        num = jnp.where(lo, a_lo[:, 0:LANES], a_hi[:, 0:LANES])
        den = jnp.where(lo, a_lo[:, LANES:2 * LANES], a_hi[:, LANES:2 * LANES])
        o_ref[:, c * LANES:(c + 1) * LANES] = (num / den).astype(BF16)


def _gqa_attention(q, k, v, batch, seq):
    ncols = B_Q_WIDTH // LANES
    tq = min(_attn_rows(seq) // (2 * ncols), seq)
    rows = 2 * ncols * tq
    nq = seq // tq
    return pl.pallas_call(
        functools.partial(_gqa_kernel, tq=tq),
        grid=(batch, nq),
        in_specs=[
            _SMEM_SPEC,
            pl.BlockSpec((tq, B_Q_WIDTH), lambda b, i: (b * nq + i, 0)),
            pl.BlockSpec((seq, B_KV_WIDTH), lambda b, i: (b, 0)),
            pl.BlockSpec((seq, B_KV_WIDTH), lambda b, i: (b, 0)),
        ],
        out_specs=pl.BlockSpec((tq, B_Q_WIDTH), lambda b, i: (b * nq + i, 0)),
        out_shape=jax.ShapeDtypeStruct(q.shape, BF16),
        scratch_shapes=_attn_scratch(rows, seq),
        compiler_params=_params("parallel", "arbitrary"),
        name="gqa_attention",
    )(_sweep_trips(rows), q, k, v)


def _diff_kernel(trips_ref, q_ref, k_ref, v_ref, lq1_ref, lk1_ref, lq2_ref, lk2_ref, g_ref, o_ref,
                 qm_ref, s_ref, m_ref, vext_ref, acc_ref, *, tq, lam_init):
    _stack_masked_heads(q_ref, qm_ref, 1, tq)
    _softmax_pv(pl.program_id(2) == 0, trips_ref, qm_ref, k_ref, v_ref, s_ref, m_ref, vext_ref, acc_ref)
    o = acc_ref[:, 0:LANES] / acc_ref[:, LANES:2 * LANES]
    lam = (jnp.exp(jnp.sum(lq1_ref[...] * lk1_ref[...], axis=-1, keepdims=True))
           - jnp.exp(jnp.sum(lq2_ref[...] * lk2_ref[...], axis=-1, keepdims=True)) + lam_init)
    d = o[0:tq] - lam * o[tq:2 * tq]
    o_ref[...] = (_rmsnorm_rows(d, g_ref[...], SUBLN_EPS) * (1.0 - lam_init)).astype(BF16)


def _diff_attention(q, k, v, lq1, lk1, lq2, lk2, g, batch, seq, lam_init):
    tq = min(_attn_rows(seq) // 2, seq)
    rows = 2 * tq
    nq = seq // tq
    vec = pl.BlockSpec((1, HEAD_DIM), lambda b, h, i: (0, 0))
    return pl.pallas_call(
        functools.partial(_diff_kernel, tq=tq, lam_init=lam_init),
        grid=(batch, DIFF_HEADS, nq),
        in_specs=[
            _SMEM_SPEC,
            pl.BlockSpec((tq, LANES), lambda b, h, i: (b * nq + i, h)),
            pl.BlockSpec((seq, LANES), lambda b, h, i: (b, h)),
            pl.BlockSpec((seq, LANES), lambda b, h, i: (b, h)),
            vec, vec, vec, vec,
            pl.BlockSpec((1, LANES), lambda b, h, i: (0, 0)),
        ],
        out_specs=pl.BlockSpec((tq, LANES), lambda b, h, i: (b * nq + i, h)),
        out_shape=jax.ShapeDtypeStruct(q.shape, BF16),
        scratch_shapes=_attn_scratch(rows, seq),
        compiler_params=_params("parallel", "parallel", "arbitrary"),
        name="diff_attention",
    )(_sweep_trips(rows), q, k, v, lq1, lk1, lq2, lk2, g)


def _na_band_start(n):
    return int(np.clip(n * NA_QCOLS - NA_WIN_COLS // 2, 0, GRID_W - NA_KCOLS))


def _na_bias_indices():
    qr = np.arange(NA_QROWS)[:, None]
    kr = np.arange(NA_KROWS)[None, :]
    half = NA_WIN_ROWS // 2
    key_shift = [0, -half, -(NA_KROWS - NA_QROWS)]
    win_start = [np.maximum(qr - half, 0), qr, np.minimum(qr + half, NA_KROWS - NA_WIN_ROWS)]
    drow = np.stack([np.clip(kr + shift - qr + NA_WIN_ROWS - 1, 0, 2 * NA_WIN_ROWS - 2) for shift in key_shift])
    rmask = np.stack([(kr >= w0) & (kr < w0 + NA_WIN_ROWS) for w0 in win_start])
    qc = np.arange(NA_QCOLS)[:, None]
    kc = np.arange(NA_KCOLS)[None, :]
    dcol, cmask = [], []
    for n in range(GRID_W // NA_QCOLS):
        qabs = n * NA_QCOLS + qc
        kabs = _na_band_start(n) + kc
        w0 = np.clip(qabs - NA_WIN_COLS // 2, 0, GRID_W - NA_WIN_COLS)
        dcol.append(np.clip(kabs - qabs + NA_WIN_COLS - 1, 0, 2 * NA_WIN_COLS - 2))
        cmask.append((kabs >= w0) & (kabs < w0 + NA_WIN_COLS))
    return drow, rmask, np.stack(dcol), np.stack(cmask)


def _na_bias_table(rpb):
    drow, rmask, dcol, cmask = _na_bias_indices()
    oh_r = jnp.asarray(np.eye(2 * NA_WIN_ROWS - 1, dtype=np.float32)[drow])
    oh_c = jnp.asarray(np.eye(2 * NA_WIN_COLS - 1, dtype=np.float32)[dcol])
    hp = lax.Precision.HIGHEST
    a = jnp.einsum("hab,vqka->hvqkb", rpb.astype(F32), oh_r, precision=hp)
    t = jnp.einsum("hvqkb,nxyb->vnhqxky", a, oh_c, precision=hp)
    mask = rmask[:, None, None, :, None, :, None] & cmask[None, :, None, None, :, None, :]
    t = jnp.where(jnp.asarray(mask), t * LOG2E, -jnp.inf)
    return t.reshape(t.shape[:3] + (NA_QROWS * NA_QCOLS, NA_KROWS * NA_KCOLS))


def _na_kernel(q_ref, k0, k1, k2, k3, v0, v1, v2, v3, tbl_ref, o_ref):
    kps = (k0, k1, k2, k3)
    vps = (v0, v1, v2, v3)
    lo = _lane_lo_mask()
    nq = NA_QROWS * NA_QCOLS
    for n in range(GRID_W // NA_QCOLS):
        band = _na_band_start(n)
        for c in range(A_WIDTH // LANES):
            lanes = slice(c * LANES, (c + 1) * LANES)
            qs = jnp.concatenate(
                [q_ref[r * GRID_W + n * NA_QCOLS:r * GRID_W + (n + 1) * NA_QCOLS, lanes] for r in range(NA_QROWS)],
                axis=0)
            zero = jnp.zeros_like(qs)
            qm = jnp.concatenate([jnp.where(lo, qs, zero), jnp.where(lo, zero, qs)], axis=0)

            def band_rows(pieces):
                return jnp.concatenate(
                    [p[r * GRID_W + band:r * GRID_W + band + NA_KCOLS, lanes]
                     for p in pieces for r in range(NA_KPIECE)], axis=0).astype(BF16)

            kb = band_rows(kps)
            vb = band_rows(vps)
            s = _dot_nt(qm, kb)
            s = s + jnp.concatenate([tbl_ref[0, n, 2 * c], tbl_ref[0, n, 2 * c + 1]], axis=0)
            m = jnp.max(s, axis=-1, keepdims=True)
            p = jnp.exp2(s - m)
            l = jnp.sum(p, axis=-1, keepdims=True)
            o = _dot(p.astype(BF16), vb) / l
            res = jnp.where(lo, o[0:nq], o[nq:2 * nq]).astype(BF16)
            for r in range(NA_QROWS):
                o_ref[r * GRID_W + n * NA_QCOLS:r * GRID_W + (n + 1) * NA_QCOLS, lanes] = (
                    res[r * NA_QCOLS:(r + 1) * NA_QCOLS])


def _na_attention(q, k, v, tbl, batch, seq):
    grid_rows = seq // GRID_W
    steps = grid_rows // NA_QROWS
    tq = NA_QROWS * GRID_W
    tp = NA_KPIECE * GRID_W
    pieces_per_batch = grid_rows // NA_KPIECE
    npieces = NA_KROWS // NA_KPIECE
    shift = (NA_WIN_ROWS // 2) // NA_KPIECE

    def piece_map(i):
        def index(b, t):
            start = jnp.clip(t * (NA_QROWS // NA_KPIECE) - shift, 0, pieces_per_batch - npieces)
            return (b * pieces_per_batch + start + i, 0)
        return index

    def tbl_map(b, t):
        variant = jnp.where(t == 0, 0, jnp.where(t == steps - 1, 2, 1))
        return (variant, 0, 0, 0, 0)

    piece_specs = [pl.BlockSpec((tp, A_WIDTH), piece_map(i)) for i in range(npieces)]
    return pl.pallas_call(
        _na_kernel,
        grid=(batch, steps),
        in_specs=[pl.BlockSpec((tq, A_WIDTH), lambda b, t: (b * steps + t, 0))] + piece_specs + piece_specs + [
            pl.BlockSpec((1,) + tbl.shape[1:], tbl_map)],
        out_specs=pl.BlockSpec((tq, A_WIDTH), lambda b, t: (b * steps + t, 0)),
        out_shape=jax.ShapeDtypeStruct(q.shape, BF16),
        compiler_params=_params("parallel", "arbitrary"),
        name="na_attention",
    )(q, k, k, k, k, v, v, v, v, tbl)


def _mlp_kernel(*refs, n_attn, final):
    x_ref = refs[0]
    attn_refs = refs[1:1 + n_attn]
    wo_ref, g_ref, wup_ref, wdown_ref, gf_ref, o_ref, h_ref, acc_ref = refs[1 + n_attn:]
    x = x_ref[...]
    width = wo_ref.shape[0] // n_attn
    for i, a_ref in enumerate(attn_refs):
        x = x + _dot(a_ref[...], wo_ref[i * width:(i + 1) * width, :])
    h = _rmsnorm_rows(x, g_ref[...], NORM_EPS).astype(BF16)
    h_ref[...] = h

    def hidden(hh, j):
        u = jnp.maximum(_dot(hh, wup_ref[j]), 0.0)
        return _dot((u * u).astype(BF16), wdown_ref[j])

    acc_ref[...] = x + hidden(h, 0)

    def hidden_chunk(j, carry):
        acc_ref[...] += hidden(h_ref[...], j)
        return carry

    last = wup_ref.shape[0] - 1
    lax.fori_loop(1, last, hidden_chunk, 0)
    y = acc_ref[...] + hidden(h_ref[...], last)
    if final:
        y = _rmsnorm_rows(y, gf_ref[...], NORM_EPS)
    o_ref[...] = y


def _outproj_mlp(x, attn, wo, g, wup, wdown, gf, final):
    n = x.shape[0]
    tm = min(TM_MLP, n)
    row = lambda i: (i, 0)
    const2 = lambda i: (0, 0)
    const3 = lambda i: (0, 0, 0)
    return pl.pallas_call(
        functools.partial(_mlp_kernel, n_attn=len(attn), final=final),
        grid=(n // tm,),
        in_specs=[pl.BlockSpec((tm, D_MODEL), row)]
        + [pl.BlockSpec((tm, a.shape[1]), row) for a in attn]
        + [
            pl.BlockSpec(wo.shape, const2),
            pl.BlockSpec((1, D_MODEL), const2),
            pl.BlockSpec(wup.shape, const3),
            pl.BlockSpec(wdown.shape, const3),
            pl.BlockSpec((1, D_MODEL), const2),
        ],
        out_specs=pl.BlockSpec((tm, D_MODEL), row),
        out_shape=jax.ShapeDtypeStruct(x.shape, F32),
        scratch_shapes=[pltpu.VMEM((tm, D_MODEL), BF16), pltpu.VMEM((tm, D_MODEL), F32)],
        compiler_params=_params("parallel"),
        name="outproj_mlp",
    )(x, *attn, wo, g, wup, wdown, gf)


def _rope_angles(pos, dim, theta):
    inv_freq = 1.0 / jnp.power(theta, jnp.arange(0, dim, 2, dtype=F32) / dim)
    ang = pos.astype(F32)[:, None] * inv_freq[None, :]
    return jnp.cos(ang), jnp.sin(ang)


def _axial_tables(seq):
    t = jnp.arange(seq)
    half = HEAD_DIM // 2
    cr, sr = _rope_angles(t // GRID_W, half, AXIAL_THETA)
    cc, sc = _rope_angles(t % GRID_W, half, AXIAL_THETA)
    cos = jnp.concatenate([cr, cr, cc, cc], axis=-1)
    sin = jnp.concatenate([-sr, sr, -sc, sc], axis=-1)
    return jnp.tile(cos, (1, LANES // HEAD_DIM)), jnp.tile(sin, (1, LANES // HEAD_DIM))


def _rope_tables(seq):
    c, s = _rope_angles(jnp.arange(seq), HEAD_DIM, ROPE_THETA)
    cos = jnp.concatenate([c, c], axis=-1)
    sin = jnp.concatenate([-s, s], axis=-1)
    return jnp.tile(cos, (1, LANES // HEAD_DIM)), jnp.tile(sin, (1, LANES // HEAD_DIM))


_GQA_HEAD_ORDER = np.arange(GQA_Q_HEADS).reshape(GQA_KV_HEADS, -1).T.reshape(-1)


def _prepare_even(w_in, q_norm, k_norm, w_out):
    a, bq = A_WIDTH, B_Q_WIDTH
    wqb = w_in[:, 3 * a:3 * a + bq].reshape(w_in.shape[0], GQA_Q_HEADS, HEAD_DIM)[:, _GQA_HEAD_ORDER, :]
    w = jnp.concatenate([w_in[:, :3 * a], wqb.reshape(w_in.shape[0], bq), w_in[:, 3 * a + bq:]], axis=1).astype(BF16)
    reps = LANES // HEAD_DIM
    gq = jnp.tile(q_norm, reps)[None]
    gk = jnp.tile(k_norm, reps)[None]
    wo_b = w_out[a:].reshape(GQA_Q_HEADS, HEAD_DIM, -1)[_GQA_HEAD_ORDER].reshape(bq, -1)
    wo = jnp.concatenate([w_out[:a], wo_b], axis=0).astype(BF16)
    return w, gq, gk, wo


def _group_ones():
    g = np.arange(LANES) // HEAD_DIM
    return jnp.asarray((g[:, None] == g[None, :]).astype(np.float32), dtype=BF16)


def _trunk(x3, p):
    batch, seq, _ = x3.shape
    assert seq % (NA_KROWS * GRID_W) == 0
    x = x3.reshape(batch * seq, D_MODEL)
    depth = len(p["layers"])
    for layer, lp in enumerate(p["layers"]):
        final = layer == depth - 1
        if layer % 2 == 0:
            cos, sin = _axial_tables(seq)
            qa, ka, va, qb, kb, vb = _inproj_even(x, lp["ln_mix"], lp["w_in"], p["gmat"], lp["gq"], lp["gk"],
                                                  cos, sin, seq)
            a_out = _na_attention(qa, ka, va, lp["na_bias"], batch, seq)
            b_out = _gqa_attention(qb, kb, vb, batch, seq)
            attn = [a_out, b_out]
        else:
            cos, sin = _rope_tables(seq)
            q, k, v = _inproj_odd(x, lp["ln_mix"], lp["w_in"], cos, sin, seq)
            attn = [_diff_attention(q, k, v, lp["lq1"], lp["lk1"], lp["lq2"], lp["lk2"], lp["subln"],
                                    batch, seq, lp["lam_init"])]
        x = _outproj_mlp(x, attn, lp["w_out"], lp["ln_mlp"], lp["w_up"], lp["w_down"], p["ln_f"], final)
    return x.reshape(batch, seq, D_MODEL)


def kernel(x_prompt, x_sample, ln_mix_e, w_in_e, rpb, q_norm_b, k_norm_b, w_out_e, ln_mix_o, w_in_o, lambda_q1, lambda_k1, lambda_q2, lambda_k2, subln_g, w_out_o, ln_mlp, w_up, w_down, ln_f):
    depth = ln_mlp.shape[0]
    layers = []
    for layer in range(depth):
        j = layer // 2
        lp = {
            "ln_mlp": ln_mlp[layer][None],
            "w_up": w_up[layer].astype(BF16).reshape(D_MODEL, D_FF // TF_MLP, TF_MLP).transpose(1, 0, 2),
            "w_down": w_down[layer].astype(BF16).reshape(D_FF // TF_MLP, TF_MLP, D_MODEL),
        }
        if layer % 2 == 0:
            w, gq, gk, wo = _prepare_even(w_in_e[j], q_norm_b[j], k_norm_b[j], w_out_e[j])
            lp.update(ln_mix=ln_mix_e[j][None], w_in=w, gq=gq, gk=gk, w_out=wo,
                      na_bias=_na_bias_table(rpb[j]))
        else:
            lp.update(ln_mix=ln_mix_o[j][None], w_in=w_in_o[j].astype(BF16), w_out=w_out_o[j].astype(BF16),
                      lq1=lambda_q1[j][None], lk1=lambda_k1[j][None], lq2=lambda_q2[j][None],
                      lk2=lambda_k2[j][None], subln=subln_g[j][None],
                      lam_init=0.8 - 0.6 * math.exp(-0.3 * layer))
        layers.append(lp)
    p = {"layers": layers, "gmat": _group_ones(), "ln_f": ln_f[None]}
    return (_trunk(x_prompt, p), _trunk(x_sample, p))
```

```python
import functools
import math

import numpy as np
import jax
import jax.numpy as jnp
from jax import lax
from jax.experimental import pallas as pl
from jax.experimental.pallas import tpu as pltpu

F32 = jnp.float32
BF16 = jnp.bfloat16

D_MODEL = 1024
HEAD_DIM = 64
GRID_W = 64
NA_HEADS = 8
NA_WIN_ROWS = 8
NA_WIN_COLS = 16
NA_QCOLS = 16
NA_KCOLS = 32
GQA_Q_HEADS = 8
GQA_KV_HEADS = 2
AXIAL_THETA = 10000.0
DIFF_HEADS = 8
D_FF = 4 * D_MODEL
ROPE_THETA = 10000.0
NORM_EPS = 1e-6
QK_NORM_EPS = 1e-6
SUBLN_EPS = 1e-5
A_WIDTH = NA_HEADS * HEAD_DIM
B_Q_WIDTH = GQA_Q_HEADS * HEAD_DIM
B_KV_WIDTH = GQA_KV_HEADS * HEAD_DIM
DIFF_WIDTH = 2 * DIFF_HEADS * HEAD_DIM
SM_SCALE = HEAD_DIM ** -0.5
LOG2E = math.log2(math.e)
Q_SCALE = SM_SCALE * LOG2E

LANES = 128
VMEM_LIMIT = 56 * 1024 * 1024

TM_PROJ = 1024
TM_MLP = 1024
TF_MLP = 1024
SCORE_SCRATCH_BYTES = 32 * 1024 * 1024
ROWS_PV = 1024
NA_QROWS = 8
NA_KROWS = 16
NA_KPIECE = 4


def _params(*sem):
    return pltpu.CompilerParams(dimension_semantics=sem, vmem_limit_bytes=VMEM_LIMIT)


def _rmsnorm_rows(x, g, eps):
    ms = jnp.mean(x * x, axis=-1, keepdims=True)
    return x * lax.rsqrt(ms + eps) * g


def _dot(a, b):
    return jnp.dot(a, b, preferred_element_type=F32)


def _dot_nt(a, b):
    return lax.dot_general(a, b, (((1,), (1,)), ((), ())), preferred_element_type=F32)


def _lane_lo_mask():
    return lax.broadcasted_iota(jnp.int32, (1, LANES), 1) < HEAD_DIM


def _swap_halves(x, span):
    first = lax.broadcasted_iota(jnp.int32, (1, LANES), 1) % (2 * span) < span
    return jnp.where(first, pltpu.roll(x, LANES - span, 1), pltpu.roll(x, span, 1))


def _group_sumsq(x, gmat_ref):
    sq = x * x
    hi = sq.astype(BF16)
    lo = (sq - hi.astype(F32)).astype(BF16)
    return _dot(hi, gmat_ref[...]) + _dot(lo, gmat_ref[...])


def _inproj_even_kernel(x_ref, g_ref, w_ref, gmat_ref, gq_ref, gk_ref, cos_ref, sin_ref,
                        qa_ref, ka_ref, va_ref, qb_ref, kb_ref, vb_ref):
    h = _rmsnorm_rows(x_ref[...], g_ref[...], NORM_EPS).astype(BF16)
    a, bq, bkv = A_WIDTH, B_Q_WIDTH, B_KV_WIDTH
    qa_ref[...] = (_dot(h, w_ref[:, 0:a]) * Q_SCALE).astype(BF16)
    ka_ref[...] = _dot(h, w_ref[:, a:2 * a])
    va_ref[...] = _dot(h, w_ref[:, 2 * a:3 * a])
    o = 3 * a
    q = _dot(h, w_ref[:, o:o + bq])
    cos = cos_ref[...]
    sin = sin_ref[...]

    def norm_rope(x, g):
        r = lax.rsqrt(_group_sumsq(x, gmat_ref) * (1.0 / HEAD_DIM) + QK_NORM_EPS)
        xg = x * g
        return r * (xg * cos + _swap_halves(xg, HEAD_DIM // 4) * sin)

    for c in range(bq // LANES):
        sl = slice(c * LANES, (c + 1) * LANES)
        qb_ref[:, sl] = (norm_rope(q[:, sl], gq_ref[...]) * Q_SCALE).astype(BF16)
    o += bq
    kb_ref[...] = norm_rope(_dot(h, w_ref[:, o:o + bkv]), gk_ref[...]).astype(BF16)
    o += bkv
    vb_ref[...] = _dot(h, w_ref[:, o:o + bkv]).astype(BF16)


def _inproj_even(x, g, w, gmat, gq, gk, cos, sin, seq):
    n = x.shape[0]
    tm = min(TM_PROJ, seq)
    pos_blocks = seq // tm
    row = lambda i: (i, 0)
    const = lambda i: (0, 0)
    pos = lambda i: (i % pos_blocks, 0)
    wcols = w.shape[1]
    return pl.pallas_call(
        _inproj_even_kernel,
        grid=(n // tm,),
        in_specs=[
            pl.BlockSpec((tm, D_MODEL), row),
            pl.BlockSpec((1, D_MODEL), const),
            pl.BlockSpec((D_MODEL, wcols), const),
            pl.BlockSpec((LANES, LANES), const),
            pl.BlockSpec((1, LANES), const),
            pl.BlockSpec((1, LANES), const),
            pl.BlockSpec((tm, LANES), pos),
            pl.BlockSpec((tm, LANES), pos),
        ],
        out_specs=[
            pl.BlockSpec((tm, A_WIDTH), row),
            pl.BlockSpec((tm, A_WIDTH), row),
            pl.BlockSpec((tm, A_WIDTH), row),
            pl.BlockSpec((tm, B_Q_WIDTH), row),
            pl.BlockSpec((tm, B_KV_WIDTH), row),
            pl.BlockSpec((tm, B_KV_WIDTH), row),
        ],
        out_shape=[
            jax.ShapeDtypeStruct((n, A_WIDTH), BF16),
            jax.ShapeDtypeStruct((n, A_WIDTH), F32),
            jax.ShapeDtypeStruct((n, A_WIDTH), F32),
            jax.ShapeDtypeStruct((n, B_Q_WIDTH), BF16),
            jax.ShapeDtypeStruct((n, B_KV_WIDTH), BF16),
            jax.ShapeDtypeStruct((n, B_KV_WIDTH), BF16),
        ],
        compiler_params=_params("parallel"),
        name="inproj_even",
    )(x, g, w, gmat, gq, gk, cos, sin)


def _inproj_odd_kernel(x_ref, g_ref, w_ref, cos_ref, sin_ref, q_ref, k_ref, v_ref):
    h = _rmsnorm_rows(x_ref[...], g_ref[...], NORM_EPS).astype(BF16)
    d = DIFF_WIDTH
    cos = cos_ref[...]
    sin = sin_ref[...]

    def rope(o, scale, out_ref):
        x = _dot(h, w_ref[:, o:o + d])
        for c in range(d // LANES):
            xc = x[:, c * LANES:(c + 1) * LANES]
            out = xc * cos + _swap_halves(xc, HEAD_DIM // 2) * sin
            out_ref[:, c * LANES:(c + 1) * LANES] = (out * scale).astype(BF16)

    rope(0, Q_SCALE, q_ref)
    rope(d, 1.0, k_ref)
    v_ref[...] = _dot(h, w_ref[:, 2 * d:3 * d]).astype(BF16)


def _inproj_odd(x, g, w, cos, sin, seq):
    n = x.shape[0]
    tm = min(TM_PROJ, seq)
    pos_blocks = seq // tm
    row = lambda i: (i, 0)
    const = lambda i: (0, 0)
    pos = lambda i: (i % pos_blocks, 0)
    out = jax.ShapeDtypeStruct((n, DIFF_WIDTH), BF16)
    return pl.pallas_call(
        _inproj_odd_kernel,
        grid=(n // tm,),
        in_specs=[
            pl.BlockSpec((tm, D_MODEL), row),
            pl.BlockSpec((1, D_MODEL), const),
            pl.BlockSpec((D_MODEL, w.shape[1]), const),
            pl.BlockSpec((tm, LANES), pos),
            pl.BlockSpec((tm, LANES), pos),
        ],
        out_specs=[pl.BlockSpec((tm, DIFF_WIDTH), row)] * 3,
        out_shape=[out, out, out],
        compiler_params=_params("parallel"),
        name="inproj_odd",
    )(x, g, w, cos, sin)


def _stack_masked_heads(q_ref, qm_ref, ncols, tq, half):
    lo = _lane_lo_mask()
    pieces = tq // half
    for c in range(ncols):
        for j in range(pieces):
            qc = q_ref[j * half:(j + 1) * half, c * LANES:(c + 1) * LANES]
            zero = jnp.zeros_like(qc)
            base = (c * pieces + j) * 2 * half
            qm_ref[base:base + half, :] = jnp.where(lo, qc, zero)
            qm_ref[base + half:base + 2 * half, :] = jnp.where(lo, zero, qc)


def _softmax_pv(first_tile, trips_ref, qm_ref, k_ref, v_ref, s_ref, m_ref, vext_ref, acc_ref):
    seq = k_ref.shape[0]
    rb = min(ROWS_PV, qm_ref.shape[0])

    @pl.when(first_tile)
    def _():
        vext_ref[:, 0:LANES] = v_ref[...]
        vext_ref[:, LANES:2 * LANES] = jnp.ones(v_ref.shape, BF16)

    s = _dot_nt(qm_ref[...], k_ref[...])
    s_ref[...] = s
    m = s[:, 0:LANES]
    for i in range(1, seq // LANES):
        m = jnp.maximum(m, s[:, i * LANES:(i + 1) * LANES])
    m_ref[...] = jnp.broadcast_to(jnp.max(m, axis=-1, keepdims=True), m_ref.shape)

    def values(rows):
        s = s_ref[rows, :]
        m = m_ref[rows, :]
        ps = [jnp.exp2(s[:, i * LANES:(i + 1) * LANES] - m).astype(BF16) for i in range(seq // LANES)]
        acc_ref[rows, :] = _dot(jnp.concatenate(ps, axis=1), vext_ref[...])

    def values_trip(r, carry):
        values(pl.ds(pl.multiple_of(r * rb, rb), rb))
        return carry

    lax.fori_loop(0, trips_ref[0], values_trip, 0)
    nrows = qm_ref.shape[0]
    values(slice(nrows - rb, nrows))


_SMEM_SPEC = pl.BlockSpec(memory_space=pltpu.SMEM)


def _attn_rows(seq):
    return SCORE_SCRATCH_BYTES // (4 * seq)


def _sweep_trips(rows):
    return jnp.asarray([rows // min(ROWS_PV, rows) - 1], jnp.int32)


def _attn_scratch(rows, seq):
    return [
        pltpu.VMEM((rows, LANES), BF16),
        pltpu.VMEM((rows, seq), F32),
        pltpu.VMEM((rows, LANES), F32),
        pltpu.VMEM((seq, 2 * LANES), BF16),
        pltpu.VMEM((rows, 2 * LANES), F32),
    ]


def _gqa_kernel(trips_ref, q_ref, k_ref, v_ref, o_ref, qm_ref, s_ref, m_ref, vext_ref, acc_ref, *, tq):
    ncols = B_Q_WIDTH // LANES
    _stack_masked_heads(q_ref, qm_ref, ncols, tq, tq)
    _softmax_pv(pl.program_id(1) == 0, trips_ref, qm_ref, k_ref, v_ref, s_ref, m_ref, vext_ref, acc_ref)
    lo = _lane_lo_mask()
    for c in range(ncols):
        a_lo = acc_ref[2 * c * tq:(2 * c + 1) * tq, :]
        a_hi = acc_ref[(2 * c + 1) * tq:(2 * c + 2) * tq, :]
        num = jnp.where(lo, a_lo[:, 0:LANES], a_hi[:, 0:LANES])
        den = jnp.where(lo, a_lo[:, LANES:2 * LANES], a_hi[:, LANES:2 * LANES])
        o_ref[:, c * LANES:(c + 1) * LANES] = (num / den).astype(BF16)


def _gqa_attention(q, k, v, batch, seq):
    ncols = B_Q_WIDTH // LANES
    tq = min(_attn_rows(seq) // (2 * ncols), seq)
    rows = 2 * ncols * tq
    nq = seq // tq
    return pl.pallas_call(
        functools.partial(_gqa_kernel, tq=tq),
        grid=(batch, nq),
        in_specs=[
            _SMEM_SPEC,
            pl.BlockSpec((tq, B_Q_WIDTH), lambda b, i: (b * nq + i, 0)),
            pl.BlockSpec((seq, B_KV_WIDTH), lambda b, i: (b, 0)),
            pl.BlockSpec((seq, B_KV_WIDTH), lambda b, i: (b, 0)),
        ],
        out_specs=pl.BlockSpec((tq, B_Q_WIDTH), lambda b, i: (b * nq + i, 0)),
        out_shape=jax.ShapeDtypeStruct(q.shape, BF16),
        scratch_shapes=_attn_scratch(rows, seq),
        compiler_params=_params("parallel", "arbitrary"),
        name="gqa_attention",
    )(_sweep_trips(rows), q, k, v)


def _diff_kernel(trips_ref, q_ref, k_ref, v_ref, lq1_ref, lk1_ref, lq2_ref, lk2_ref, g_ref, o_ref,
                 qm_ref, s_ref, m_ref, vext_ref, acc_ref, *, tq, lam_init):
    half = min(ROWS_PV // 2, tq)
    _stack_masked_heads(q_ref, qm_ref, 1, tq, half)
    _softmax_pv(pl.program_id(2) == 0, trips_ref, qm_ref, k_ref, v_ref, s_ref, m_ref, vext_ref, acc_ref)
    lam = (jnp.exp(jnp.sum(lq1_ref[...] * lk1_ref[...], axis=-1, keepdims=True))
           - jnp.exp(jnp.sum(lq2_ref[...] * lk2_ref[...], axis=-1, keepdims=True)) + lam_init)
    for j in range(tq // half):
        a = acc_ref[2 * j * half:(2 * j + 2) * half, :]
        o = a[:, 0:LANES] / a[:, LANES:2 * LANES]
        d = o[0:half] - lam * o[half:2 * half]
        o_ref[j * half:(j + 1) * half, :] = (_rmsnorm_rows(d, g_ref[...], SUBLN_EPS) * (1.0 - lam_init)).astype(BF16)


def _diff_attention(q, k, v, lq1, lk1, lq2, lk2, g, batch, seq, lam_init):
    tq = min(_attn_rows(seq) // 2, seq)
    rows = 2 * tq
    nq = seq // tq
    vec = pl.BlockSpec((1, HEAD_DIM), lambda b, h, i: (0, 0))
    return pl.pallas_call(
        functools.partial(_diff_kernel, tq=tq, lam_init=lam_init),
        grid=(batch, DIFF_HEADS, nq),
        in_specs=[
            _SMEM_SPEC,
            pl.BlockSpec((tq, LANES), lambda b, h, i: (b * nq + i, h)),
            pl.BlockSpec((seq, LANES), lambda b, h, i: (b, h)),
            pl.BlockSpec((seq, LANES), lambda b, h, i: (b, h)),
            vec, vec, vec, vec,
            pl.BlockSpec((1, LANES), lambda b, h, i: (0, 0)),
        ],
        out_specs=pl.BlockSpec((tq, LANES), lambda b, h, i: (b * nq + i, h)),
        out_shape=jax.ShapeDtypeStruct(q.shape, BF16),
        scratch_shapes=_attn_scratch(rows, seq),
        compiler_params=_params("parallel", "parallel", "arbitrary"),
        name="diff_attention",
    )(_sweep_trips(rows), q, k, v, lq1, lk1, lq2, lk2, g)


def _na_band_start(n):
    return int(np.clip(n * NA_QCOLS - NA_WIN_COLS // 2, 0, GRID_W - NA_KCOLS))


def _na_bias_indices():
    qr = np.arange(NA_QROWS)[:, None]
    kr = np.arange(NA_KROWS)[None, :]
    half = NA_WIN_ROWS // 2
    key_shift = [0, -half, -(NA_KROWS - NA_QROWS)]
    win_start = [np.maximum(qr - half, 0), qr, np.minimum(qr + half, NA_KROWS - NA_WIN_ROWS)]
    drow = np.stack([np.clip(kr + shift - qr + NA_WIN_ROWS - 1, 0, 2 * NA_WIN_ROWS - 2) for shift in key_shift])
    rmask = np.stack([(kr >= w0) & (kr < w0 + NA_WIN_ROWS) for w0 in win_start])
    qc = np.arange(NA_QCOLS)[:, None]
    kc = np.arange(NA_KCOLS)[None, :]
    dcol, cmask = [], []
    for n in range(GRID_W // NA_QCOLS):
        qabs = n * NA_QCOLS + qc
        kabs = _na_band_start(n) + kc
        w0 = np.clip(qabs - NA_WIN_COLS // 2, 0, GRID_W - NA_WIN_COLS)
        dcol.append(np.clip(kabs - qabs + NA_WIN_COLS - 1, 0, 2 * NA_WIN_COLS - 2))
        cmask.append((kabs >= w0) & (kabs < w0 + NA_WIN_COLS))
    return drow, rmask, np.stack(dcol), np.stack(cmask)


def _na_bias_table(rpb):
    drow, rmask, dcol, cmask = _na_bias_indices()
    oh_r = jnp.asarray(np.eye(2 * NA_WIN_ROWS - 1, dtype=np.float32)[drow])
    oh_c = jnp.asarray(np.eye(2 * NA_WIN_COLS - 1, dtype=np.float32)[dcol])
    hp = lax.Precision.HIGHEST
    a = jnp.einsum("hab,vqka->hvqkb", rpb.astype(F32), oh_r, precision=hp)
    t = jnp.einsum("hvqkb,nxyb->vnhqxky", a, oh_c, precision=hp)
    mask = rmask[:, None, None, :, None, :, None] & cmask[None, :, None, None, :, None, :]
    t = jnp.where(jnp.asarray(mask), t * LOG2E, -jnp.inf)
    return t.reshape(t.shape[:3] + (NA_QROWS * NA_QCOLS, NA_KROWS * NA_KCOLS))


def _na_kernel(q_ref, k0, k1, k2, k3, v0, v1, v2, v3, tbl_ref, o_ref):
    kps = (k0, k1, k2, k3)
    vps = (v0, v1, v2, v3)
    lo = _lane_lo_mask()
    nq = NA_QROWS * NA_QCOLS
    for n in range(GRID_W // NA_QCOLS):
        band = _na_band_start(n)
        for c in range(A_WIDTH // LANES):
            lanes = slice(c * LANES, (c + 1) * LANES)
            qs = jnp.concatenate(
                [q_ref[r * GRID_W + n * NA_QCOLS:r * GRID_W + (n + 1) * NA_QCOLS, lanes] for r in range(NA_QROWS)],
                axis=0)
            zero = jnp.zeros_like(qs)
            qm = jnp.concatenate([jnp.where(lo, qs, zero), jnp.where(lo, zero, qs)], axis=0)

            def band_rows(pieces):
                return jnp.concatenate(
                    [p[r * GRID_W + band:r * GRID_W + band + NA_KCOLS, lanes]
                     for p in pieces for r in range(NA_KPIECE)], axis=0).astype(BF16)

            kb = band_rows(kps)
            vb = band_rows(vps)
            s = _dot_nt(qm, kb)
            s = s + jnp.concatenate([tbl_ref[0, n, 2 * c], tbl_ref[0, n, 2 * c + 1]], axis=0)
            m = jnp.max(s, axis=-1, keepdims=True)
            p = jnp.exp2(s - m)
            l = jnp.sum(p, axis=-1, keepdims=True)
            o = _dot(p.astype(BF16), vb) / l
            res = jnp.where(lo, o[0:nq], o[nq:2 * nq]).astype(BF16)
            for r in range(NA_QROWS):
                o_ref[r * GRID_W + n * NA_QCOLS:r * GRID_W + (n + 1) * NA_QCOLS, lanes] = (
                    res[r * NA_QCOLS:(r + 1) * NA_QCOLS])


def _na_attention(q, k, v, tbl, batch, seq):
    grid_rows = seq // GRID_W
    steps = grid_rows // NA_QROWS
    tq = NA_QROWS * GRID_W
    tp = NA_KPIECE * GRID_W
    pieces_per_batch = grid_rows // NA_KPIECE
    npieces = NA_KROWS // NA_KPIECE
    shift = (NA_WIN_ROWS // 2) // NA_KPIECE

    def piece_map(i):
        def index(b, t):
            start = jnp.clip(t * (NA_QROWS // NA_KPIECE) - shift, 0, pieces_per_batch - npieces)
            return (b * pieces_per_batch + start + i, 0)
        return index

    def tbl_map(b, t):
        variant = jnp.where(t == 0, 0, jnp.where(t == steps - 1, 2, 1))
        return (variant, 0, 0, 0, 0)

    piece_specs = [pl.BlockSpec((tp, A_WIDTH), piece_map(i)) for i in range(npieces)]
    return pl.pallas_call(
        _na_kernel,
        grid=(batch, steps),
        in_specs=[pl.BlockSpec((tq, A_WIDTH), lambda b, t: (b * steps + t, 0))] + piece_specs + piece_specs + [
            pl.BlockSpec((1,) + tbl.shape[1:], tbl_map)],
        out_specs=pl.BlockSpec((tq, A_WIDTH), lambda b, t: (b * steps + t, 0)),
        out_shape=jax.ShapeDtypeStruct(q.shape, BF16),
        compiler_params=_params("parallel", "arbitrary"),
        name="na_attention",
    )(q, k, k, k, k, v, v, v, v, tbl)


def _mlp_kernel(*refs, n_attn, final):
    x_ref = refs[0]
    attn_refs = refs[1:1 + n_attn]
    wo_ref, g_ref, wup_ref, wdown_ref, gf_ref, o_ref, h_ref, acc_ref = refs[1 + n_attn:]
    x = x_ref[...]
    width = wo_ref.shape[0] // n_attn
    for i, a_ref in enumerate(attn_refs):
        x = x + _dot(a_ref[...], wo_ref[i * width:(i + 1) * width, :])
    h = _rmsnorm_rows(x, g_ref[...], NORM_EPS).astype(BF16)
    h_ref[...] = h

    def hidden(hh, j):
        u = jnp.maximum(_dot(hh, wup_ref[j]), 0.0)
        return _dot((u * u).astype(BF16), wdown_ref[j])

    acc_ref[...] = x + hidden(h, 0)

    def hidden_chunk(j, carry):
        acc_ref[...] += hidden(h_ref[...], j)
        return carry

    last = wup_ref.shape[0] - 1
    lax.fori_loop(1, last, hidden_chunk, 0)
    y = acc_ref[...] + hidden(h_ref[...], last)
    if final:
        y = _rmsnorm_rows(y, gf_ref[...], NORM_EPS)
    o_ref[...] = y


def _outproj_mlp(x, attn, wo, g, wup, wdown, gf, final):
    n = x.shape[0]
    tm = min(TM_MLP, n)
    row = lambda i: (i, 0)
    const2 = lambda i: (0, 0)
    const3 = lambda i: (0, 0, 0)
    return pl.pallas_call(
        functools.partial(_mlp_kernel, n_attn=len(attn), final=final),
        grid=(n // tm,),
        in_specs=[pl.BlockSpec((tm, D_MODEL), row)]
        + [pl.BlockSpec((tm, a.shape[1]), row) for a in attn]
        + [
            pl.BlockSpec(wo.shape, const2),
            pl.BlockSpec((1, D_MODEL), const2),
            pl.BlockSpec(wup.shape, const3),
            pl.BlockSpec(wdown.shape, const3),
            pl.BlockSpec((1, D_MODEL), const2),
        ],
        out_specs=pl.BlockSpec((tm, D_MODEL), row),
        out_shape=jax.ShapeDtypeStruct(x.shape, F32),
        scratch_shapes=[pltpu.VMEM((tm, D_MODEL), BF16), pltpu.VMEM((tm, D_MODEL), F32)],
        compiler_params=_params("parallel"),
        name="outproj_mlp",
    )(x, *attn, wo, g, wup, wdown, gf)


def _rope_angles(pos, dim, theta):
    inv_freq = 1.0 / jnp.power(theta, jnp.arange(0, dim, 2, dtype=F32) / dim)
    ang = pos.astype(F32)[:, None] * inv_freq[None, :]
    return jnp.cos(ang), jnp.sin(ang)


def _axial_tables(seq):
    t = jnp.arange(seq)
    half = HEAD_DIM // 2
    cr, sr = _rope_angles(t // GRID_W, half, AXIAL_THETA)
    cc, sc = _rope_angles(t % GRID_W, half, AXIAL_THETA)
    cos = jnp.concatenate([cr, cr, cc, cc], axis=-1)
    sin = jnp.concatenate([-sr, sr, -sc, sc], axis=-1)
    return jnp.tile(cos, (1, LANES // HEAD_DIM)), jnp.tile(sin, (1, LANES // HEAD_DIM))


def _rope_tables(seq):
    c, s = _rope_angles(jnp.arange(seq), HEAD_DIM, ROPE_THETA)
    cos = jnp.concatenate([c, c], axis=-1)
    sin = jnp.concatenate([-s, s], axis=-1)
    return jnp.tile(cos, (1, LANES // HEAD_DIM)), jnp.tile(sin, (1, LANES // HEAD_DIM))


_GQA_HEAD_ORDER = np.arange(GQA_Q_HEADS).reshape(GQA_KV_HEADS, -1).T.reshape(-1)


def _prepare_even(w_in, q_norm, k_norm, w_out):
    a, bq = A_WIDTH, B_Q_WIDTH
    wqb = w_in[:, 3 * a:3 * a + bq].reshape(w_in.shape[0], GQA_Q_HEADS, HEAD_DIM)[:, _GQA_HEAD_ORDER, :]
    w = jnp.concatenate([w_in[:, :3 * a], wqb.reshape(w_in.shape[0], bq), w_in[:, 3 * a + bq:]], axis=1).astype(BF16)
    reps = LANES // HEAD_DIM
    gq = jnp.tile(q_norm, reps)[None]
    gk = jnp.tile(k_norm, reps)[None]
    wo_b = w_out[a:].reshape(GQA_Q_HEADS, HEAD_DIM, -1)[_GQA_HEAD_ORDER].reshape(bq, -1)
    wo = jnp.concatenate([w_out[:a], wo_b], axis=0).astype(BF16)
    return w, gq, gk, wo


def _group_ones():
    g = np.arange(LANES) // HEAD_DIM
    return jnp.asarray((g[:, None] == g[None, :]).astype(np.float32), dtype=BF16)


def _trunk(x3, p):
    batch, seq, _ = x3.shape
    assert seq % (NA_KROWS * GRID_W) == 0
    x = x3.reshape(batch * seq, D_MODEL)
    depth = len(p["layers"])
    for layer, lp in enumerate(p["layers"]):
        final = layer == depth - 1
        if layer % 2 == 0:
            cos, sin = _axial_tables(seq)
            qa, ka, va, qb, kb, vb = _inproj_even(x, lp["ln_mix"], lp["w_in"], p["gmat"], lp["gq"], lp["gk"],
                                                  cos, sin, seq)
            a_out = _na_attention(qa, ka, va, lp["na_bias"], batch, seq)
            b_out = _gqa_attention(qb, kb, vb, batch, seq)
            attn = [a_out, b_out]
        else:
            cos, sin = _rope_tables(seq)
            q, k, v = _inproj_odd(x, lp["ln_mix"], lp["w_in"], cos, sin, seq)
            attn = [_diff_attention(q, k, v, lp["lq1"], lp["lk1"], lp["lq2"], lp["lk2"], lp["subln"],
                                    batch, seq, lp["lam_init"])]
        x = _outproj_mlp(x, attn, lp["w_out"], lp["ln_mlp"], lp["w_up"], lp["w_down"], p["ln_f"], final)
    return x.reshape(batch, seq, D_MODEL)


def kernel(x_prompt, x_sample, ln_mix_e, w_in_e, rpb, q_norm_b, k_norm_b, w_out_e, ln_mix_o, w_in_o, lambda_q1, lambda_k1, lambda_q2, lambda_k2, subln_g, w_out_o, ln_mlp, w_up, w_down, ln_f):
    depth = ln_mlp.shape[0]
    layers = []
    for layer in range(depth):
        j = layer // 2
        lp = {
            "ln_mlp": ln_mlp[layer][None],
            "w_up": w_up[layer].astype(BF16).reshape(D_MODEL, D_FF // TF_MLP, TF_MLP).transpose(1, 0, 2),
            "w_down": w_down[layer].astype(BF16).reshape(D_FF // TF_MLP, TF_MLP, D_MODEL),
        }
        if layer % 2 == 0:
            w, gq, gk, wo = _prepare_even(w_in_e[j], q_norm_b[j], k_norm_b[j], w_out_e[j])
            lp.update(ln_mix=ln_mix_e[j][None], w_in=w, gq=gq, gk=gk, w_out=wo,
                      na_bias=_na_bias_table(rpb[j]))
        else:
            lp.update(ln_mix=ln_mix_o[j][None], w_in=w_in_o[j].astype(BF16), w_out=w_out_o[j].astype(BF16),
                      lq1=lambda_q1[j][None], lk1=lambda_k1[j][None], lq2=lambda_q2[j][None],
                      lk2=lambda_k2[j][None], subln=subln_g[j][None],
                      lam_init=0.8 - 0.6 * math.exp(-0.3 * layer))
        layers.append(lp)
    p = {"layers": layers, "gmat": _group_ones(), "ln_f": ln_f[None]}
    return (_trunk(x_prompt, p), _trunk(x_sample, p))
```

```python
import functools
import math

import numpy as np
import jax
import jax.numpy as jnp
from jax import lax
from jax.experimental import pallas as pl
from jax.experimental.pallas import tpu as pltpu

F32 = jnp.float32
BF16 = jnp.bfloat16

D_MODEL = 1024
HEAD_DIM = 64
GRID_W = 64
NA_HEADS = 8
NA_WIN_ROWS = 8
NA_WIN_COLS = 16
NA_QCOLS = 16
NA_KCOLS = 32
GQA_Q_HEADS = 8
GQA_KV_HEADS = 2
AXIAL_THETA = 10000.0
DIFF_HEADS = 8
D_FF = 4 * D_MODEL
ROPE_THETA = 10000.0
NORM_EPS = 1e-6
QK_NORM_EPS = 1e-6
SUBLN_EPS = 1e-5
A_WIDTH = NA_HEADS * HEAD_DIM
B_Q_WIDTH = GQA_Q_HEADS * HEAD_DIM
B_KV_WIDTH = GQA_KV_HEADS * HEAD_DIM
DIFF_WIDTH = 2 * DIFF_HEADS * HEAD_DIM
SM_SCALE = HEAD_DIM ** -0.5
LOG2E = math.log2(math.e)
Q_SCALE = SM_SCALE * LOG2E

LANES = 128
VMEM_LIMIT = 56 * 1024 * 1024

TM_PROJ = 1024
TM_MLP = 1024
TF_MLP = 1024
SCORE_SCRATCH_BYTES = 32 * 1024 * 1024
ROWS_PV = 1024
NA_QROWS = 8
NA_KROWS = 16
NA_KPIECE = 4


def _params(*sem):
    return pltpu.CompilerParams(dimension_semantics=sem, vmem_limit_bytes=VMEM_LIMIT)


def _rmsnorm_rows(x, g, eps):
    ms = jnp.mean(x * x, axis=-1, keepdims=True)
    return x * lax.rsqrt(ms + eps) * g


def _dot(a, b):
    return jnp.dot(a, b, preferred_element_type=F32)


def _dot_nt(a, b):
    return lax.dot_general(a, b, (((1,), (1,)), ((), ())), preferred_element_type=F32)


def _lane_lo_mask():
    return lax.broadcasted_iota(jnp.int32, (1, LANES), 1) < HEAD_DIM


def _swap_halves(x, span):
    first = lax.broadcasted_iota(jnp.int32, (1, LANES), 1) % (2 * span) < span
    return jnp.where(first, pltpu.roll(x, LANES - span, 1), pltpu.roll(x, span, 1))


def _group_sumsq(x, gmat_ref):
    sq = x * x
    hi = sq.astype(BF16)
    lo = (sq - hi.astype(F32)).astype(BF16)
    return _dot(hi, gmat_ref[...]) + _dot(lo, gmat_ref[...])


def _inproj_even_kernel(x_ref, g_ref, w_ref, gmat_ref, gq_ref, gk_ref, cos_ref, sin_ref,
                        qa_ref, ka_ref, va_ref, qb_ref, kb_ref, vb_ref):
    h = _rmsnorm_rows(x_ref[...], g_ref[...], NORM_EPS).astype(BF16)
    a, bq, bkv = A_WIDTH, B_Q_WIDTH, B_KV_WIDTH
    qa_ref[...] = (_dot(h, w_ref[:, 0:a]) * Q_SCALE).astype(BF16)
    ka_ref[...] = _dot(h, w_ref[:, a:2 * a])
    va_ref[...] = _dot(h, w_ref[:, 2 * a:3 * a])
    o = 3 * a
    q = _dot(h, w_ref[:, o:o + bq])
    cos = cos_ref[...]
    sin = sin_ref[...]

    def norm_rope(x, g):
        r = lax.rsqrt(_group_sumsq(x, gmat_ref) * (1.0 / HEAD_DIM) + QK_NORM_EPS)
        xg = x * g
        return r * (xg * cos + _swap_halves(xg, HEAD_DIM // 4) * sin)

    for c in range(bq // LANES):
        sl = slice(c * LANES, (c + 1) * LANES)
        qb_ref[:, sl] = (norm_rope(q[:, sl], gq_ref[...]) * Q_SCALE).astype(BF16)
    o += bq
    kb_ref[...] = norm_rope(_dot(h, w_ref[:, o:o + bkv]), gk_ref[...]).astype(BF16)
    o += bkv
    vb_ref[...] = _dot(h, w_ref[:, o:o + bkv]).astype(BF16)


def _inproj_even(x, g, w, gmat, gq, gk, cos, sin, seq):
    n = x.shape[0]
    tm = min(TM_PROJ, seq)
    pos_blocks = seq // tm
    row = lambda i: (i, 0)
    const = lambda i: (0, 0)
    pos = lambda i: (i % pos_blocks, 0)
    wcols = w.shape[1]
    return pl.pallas_call(
        _inproj_even_kernel,
        grid=(n // tm,),
        in_specs=[
            pl.BlockSpec((tm, D_MODEL), row),
            pl.BlockSpec((1, D_MODEL), const),
            pl.BlockSpec((D_MODEL, wcols), const),
            pl.BlockSpec((LANES, LANES), const),
            pl.BlockSpec((1, LANES), const),
            pl.BlockSpec((1, LANES), const),
            pl.BlockSpec((tm, LANES), pos),
            pl.BlockSpec((tm, LANES), pos),
        ],
        out_specs=[
            pl.BlockSpec((tm, A_WIDTH), row),
            pl.BlockSpec((tm, A_WIDTH), row),
            pl.BlockSpec((tm, A_WIDTH), row),
            pl.BlockSpec((tm, B_Q_WIDTH), row),
            pl.BlockSpec((tm, B_KV_WIDTH), row),
            pl.BlockSpec((tm, B_KV_WIDTH), row),
        ],
        out_shape=[
            jax.ShapeDtypeStruct((n, A_WIDTH), BF16),
            jax.ShapeDtypeStruct((n, A_WIDTH), F32),
            jax.ShapeDtypeStruct((n, A_WIDTH), F32),
            jax.ShapeDtypeStruct((n, B_Q_WIDTH), BF16),
            jax.ShapeDtypeStruct((n, B_KV_WIDTH), BF16),
            jax.ShapeDtypeStruct((n, B_KV_WIDTH), BF16),
        ],
        compiler_params=_params("parallel"),
        name="inproj_even",
    )(x, g, w, gmat, gq, gk, cos, sin)


def _inproj_odd_kernel(x_ref, g_ref, w_ref, cos_ref, sin_ref, q_ref, k_ref, v_ref):
    h = _rmsnorm_rows(x_ref[...], g_ref[...], NORM_EPS).astype(BF16)
    d = DIFF_WIDTH
    cos = cos_ref[...]
    sin = sin_ref[...]

    def rope(o, scale, out_ref):
        x = _dot(h, w_ref[:, o:o + d])
        for c in range(d // LANES):
            xc = x[:, c * LANES:(c + 1) * LANES]
            out = xc * cos + _swap_halves(xc, HEAD_DIM // 2) * sin
            out_ref[:, c * LANES:(c + 1) * LANES] = (out * scale).astype(BF16)

    rope(0, Q_SCALE, q_ref)
    rope(d, 1.0, k_ref)
    v_ref[...] = _dot(h, w_ref[:, 2 * d:3 * d]).astype(BF16)


def _inproj_odd(x, g, w, cos, sin, seq):
    n = x.shape[0]
    tm = min(TM_PROJ, seq)
    pos_blocks = seq // tm
    row = lambda i: (i, 0)
    const = lambda i: (0, 0)
    pos = lambda i: (i % pos_blocks, 0)
    out = jax.ShapeDtypeStruct((n, DIFF_WIDTH), BF16)
    return pl.pallas_call(
        _inproj_odd_kernel,
        grid=(n // tm,),
        in_specs=[
            pl.BlockSpec((tm, D_MODEL), row),
            pl.BlockSpec((1, D_MODEL), const),
            pl.BlockSpec((D_MODEL, w.shape[1]), const),
            pl.BlockSpec((tm, LANES), pos),
            pl.BlockSpec((tm, LANES), pos),
        ],
        out_specs=[pl.BlockSpec((tm, DIFF_WIDTH), row)] * 3,
        out_shape=[out, out, out],
        compiler_params=_params("parallel"),
        name="inproj_odd",
    )(x, g, w, cos, sin)


def _stack_masked_heads(q_ref, qm_ref, ncols, tq, half):
    lo = _lane_lo_mask()
    pieces = tq // half
    for c in range(ncols):
        for j in range(pieces):
            qc = q_ref[j * half:(j + 1) * half, c * LANES:(c + 1) * LANES]
            zero = jnp.zeros_like(qc)
            base = (c * pieces + j) * 2 * half
            qm_ref[base:base + half, :] = jnp.where(lo, qc, zero)
            qm_ref[base + half:base + 2 * half, :] = jnp.where(lo, zero, qc)


def _softmax_pv(first_tile, qm_ref, k_ref, v_ref, s_ref, m_ref, vext_ref, acc_ref):
    seq = k_ref.shape[0]
    rb = min(ROWS_PV, qm_ref.shape[0])

    @pl.when(first_tile)
    def _():
        vext_ref[:, 0:LANES] = v_ref[...]
        vext_ref[:, LANES:2 * LANES] = jnp.ones(v_ref.shape, BF16)

    s = _dot_nt(qm_ref[...], k_ref[...])
    s_ref[...] = s
    m = s[:, 0:LANES]
    for i in range(1, seq // LANES):
        m = jnp.maximum(m, s[:, i * LANES:(i + 1) * LANES])
    m_ref[...] = jnp.broadcast_to(jnp.max(m, axis=-1, keepdims=True), m_ref.shape)

    def values(rows):
        s = s_ref[rows, :]
        m = m_ref[rows, :]
        ps = [jnp.exp2(s[:, i * LANES:(i + 1) * LANES] - m).astype(BF16) for i in range(seq // LANES)]
        acc_ref[rows, :] = _dot(jnp.concatenate(ps, axis=1), vext_ref[...])

    for r in range(qm_ref.shape[0] // rb):
        values(slice(r * rb, (r + 1) * rb))


def _attn_rows(seq):
    return SCORE_SCRATCH_BYTES // (4 * seq)


def _attn_scratch(rows, seq):
    return [
        pltpu.VMEM((rows, LANES), BF16),
        pltpu.VMEM((rows, seq), F32),
        pltpu.VMEM((rows, LANES), F32),
        pltpu.VMEM((seq, 2 * LANES), BF16),
        pltpu.VMEM((rows, 2 * LANES), F32),
    ]


def _gqa_kernel(q_ref, k_ref, v_ref, o_ref, qm_ref, s_ref, m_ref, vext_ref, acc_ref, *, tq):
    ncols = B_Q_WIDTH // LANES
    _stack_masked_heads(q_ref, qm_ref, ncols, tq, tq)
    _softmax_pv(pl.program_id(1) == 0, qm_ref, k_ref, v_ref, s_ref, m_ref, vext_ref, acc_ref)
    lo = _lane_lo_mask()
    for c in range(ncols):
        a_lo = acc_ref[2 * c * tq:(2 * c + 1) * tq, :]
        a_hi = acc_ref[(2 * c + 1) * tq:(2 * c + 2) * tq, :]
        num = jnp.where(lo, a_lo[:, 0:LANES], a_hi[:, 0:LANES])
        den = jnp.where(lo, a_lo[:, LANES:2 * LANES], a_hi[:, LANES:2 * LANES])
        o_ref[:, c * LANES:(c + 1) * LANES] = (num / den).astype(BF16)


def _gqa_attention(q, k, v, batch, seq):
    ncols = B_Q_WIDTH // LANES
    tq = min(_attn_rows(seq) // (2 * ncols), seq)
    rows = 2 * ncols * tq
    nq = seq // tq
    return pl.pallas_call(
        functools.partial(_gqa_kernel, tq=tq),
        grid=(batch, nq),
        in_specs=[
            pl.BlockSpec((tq, B_Q_WIDTH), lambda b, i: (b * nq + i, 0)),
            pl.BlockSpec((seq, B_KV_WIDTH), lambda b, i: (b, 0)),
            pl.BlockSpec((seq, B_KV_WIDTH), lambda b, i: (b, 0)),
        ],
        out_specs=pl.BlockSpec((tq, B_Q_WIDTH), lambda b, i: (b * nq + i, 0)),
        out_shape=jax.ShapeDtypeStruct(q.shape, BF16),
        scratch_shapes=_attn_scratch(rows, seq),
        compiler_params=_params("parallel", "arbitrary"),
        name="gqa_attention",
    )(q, k, v)


def _diff_kernel(q_ref, k_ref, v_ref, lq1_ref, lk1_ref, lq2_ref, lk2_ref, g_ref, o_ref,
                 qm_ref, s_ref, m_ref, vext_ref, acc_ref, *, tq, lam_init):
    half = min(ROWS_PV // 2, tq)
    _stack_masked_heads(q_ref, qm_ref, 1, tq, half)
    _softmax_pv(pl.program_id(2) == 0, qm_ref, k_ref, v_ref, s_ref, m_ref, vext_ref, acc_ref)
    lam = (jnp.exp(jnp.sum(lq1_ref[...] * lk1_ref[...], axis=-1, keepdims=True))
           - jnp.exp(jnp.sum(lq2_ref[...] * lk2_ref[...], axis=-1, keepdims=True)) + lam_init)
    for j in range(tq // half):
        a = acc_ref[2 * j * half:(2 * j + 2) * half, :]
        o = a[:, 0:LANES] / a[:, LANES:2 * LANES]
        d = o[0:half] - lam * o[half:2 * half]
        o_ref[j * half:(j + 1) * half, :] = (_rmsnorm_rows(d, g_ref[...], SUBLN_EPS) * (1.0 - lam_init)).astype(BF16)


def _diff_attention(q, k, v, lq1, lk1, lq2, lk2, g, batch, seq, lam_init):
    tq = min(_attn_rows(seq) // 2, seq)
    rows = 2 * tq
    nq = seq // tq
    vec = pl.BlockSpec((1, HEAD_DIM), lambda b, h, i: (0, 0))
    return pl.pallas_call(
        functools.partial(_diff_kernel, tq=tq, lam_init=lam_init),
        grid=(batch, DIFF_HEADS, nq),
        in_specs=[
            pl.BlockSpec((tq, LANES), lambda b, h, i: (b * nq + i, h)),
            pl.BlockSpec((seq, LANES), lambda b, h, i: (b, h)),
            pl.BlockSpec((seq, LANES), lambda b, h, i: (b, h)),
            vec, vec, vec, vec,
            pl.BlockSpec((1, LANES), lambda b, h, i: (0, 0)),
        ],
        out_specs=pl.BlockSpec((tq, LANES), lambda b, h, i: (b * nq + i, h)),
        out_shape=jax.ShapeDtypeStruct(q.shape, BF16),
        scratch_shapes=_attn_scratch(rows, seq),
        compiler_params=_params("parallel", "parallel", "arbitrary"),
        name="diff_attention",
    )(q, k, v, lq1, lk1, lq2, lk2, g)


def _na_band_start(n):
    return int(np.clip(n * NA_QCOLS - NA_WIN_COLS // 2, 0, GRID_W - NA_KCOLS))


def _na_bias_indices():
    qr = np.arange(NA_QROWS)[:, None]
    kr = np.arange(NA_KROWS)[None, :]
    half = NA_WIN_ROWS // 2
    key_shift = [0, -half, -(NA_KROWS - NA_QROWS)]
    win_start = [np.maximum(qr - half, 0), qr, np.minimum(qr + half, NA_KROWS - NA_WIN_ROWS)]
    drow = np.stack([np.clip(kr + shift - qr + NA_WIN_ROWS - 1, 0, 2 * NA_WIN_ROWS - 2) for shift in key_shift])
    rmask = np.stack([(kr >= w0) & (kr < w0 + NA_WIN_ROWS) for w0 in win_start])
    qc = np.arange(NA_QCOLS)[:, None]
    kc = np.arange(NA_KCOLS)[None, :]
    dcol, cmask = [], []
    for n in range(GRID_W // NA_QCOLS):
        qabs = n * NA_QCOLS + qc
        kabs = _na_band_start(n) + kc
        w0 = np.clip(qabs - NA_WIN_COLS // 2, 0, GRID_W - NA_WIN_COLS)
        dcol.append(np.clip(kabs - qabs + NA_WIN_COLS - 1, 0, 2 * NA_WIN_COLS - 2))
        cmask.append((kabs >= w0) & (kabs < w0 + NA_WIN_COLS))
    return drow, rmask, np.stack(dcol), np.stack(cmask)


def _na_bias_table(rpb):
    drow, rmask, dcol, cmask = _na_bias_indices()
    oh_r = jnp.asarray(np.eye(2 * NA_WIN_ROWS - 1, dtype=np.float32)[drow])
    oh_c = jnp.asarray(np.eye(2 * NA_WIN_COLS - 1, dtype=np.float32)[dcol])
    hp = lax.Precision.HIGHEST
    a = jnp.einsum("hab,vqka->hvqkb", rpb.astype(F32), oh_r, precision=hp)
    t = jnp.einsum("hvqkb,nxyb->vnhqxky", a, oh_c, precision=hp)
    mask = rmask[:, None, None, :, None, :, None] & cmask[None, :, None, None, :, None, :]
    t = jnp.where(jnp.asarray(mask), t * LOG2E, -jnp.inf)
    return t.reshape(t.shape[:3] + (NA_QROWS * NA_QCOLS, NA_KROWS * NA_KCOLS))


def _na_kernel(q_ref, k0, k1, k2, k3, v0, v1, v2, v3, tbl_ref, o_ref):
    kps = (k0, k1, k2, k3)
    vps = (v0, v1, v2, v3)
    lo = _lane_lo_mask()
    nq = NA_QROWS * NA_QCOLS
    for n in range(GRID_W // NA_QCOLS):
        band = _na_band_start(n)
        for c in range(A_WIDTH // LANES):
            lanes = slice(c * LANES, (c + 1) * LANES)
            qs = jnp.concatenate(
                [q_ref[r * GRID_W + n * NA_QCOLS:r * GRID_W + (n + 1) * NA_QCOLS, lanes] for r in range(NA_QROWS)],
                axis=0)
            zero = jnp.zeros_like(qs)
            qm = jnp.concatenate([jnp.where(lo, qs, zero), jnp.where(lo, zero, qs)], axis=0)

            def band_rows(pieces):
                return jnp.concatenate(
                    [p[r * GRID_W + band:r * GRID_W + band + NA_KCOLS, lanes]
                     for p in pieces for r in range(NA_KPIECE)], axis=0).astype(BF16)

            kb = band_rows(kps)
            vb = band_rows(vps)
            s = _dot_nt(qm, kb)
            s = s + jnp.concatenate([tbl_ref[0, n, 2 * c], tbl_ref[0, n, 2 * c + 1]], axis=0)
            m = jnp.max(s, axis=-1, keepdims=True)
            p = jnp.exp2(s - m)
            l = jnp.sum(p, axis=-1, keepdims=True)
            o = _dot(p.astype(BF16), vb) / l
            res = jnp.where(lo, o[0:nq], o[nq:2 * nq]).astype(BF16)
            for r in range(NA_QROWS):
                o_ref[r * GRID_W + n * NA_QCOLS:r * GRID_W + (n + 1) * NA_QCOLS, lanes] = (
                    res[r * NA_QCOLS:(r + 1) * NA_QCOLS])


def _na_attention(q, k, v, tbl, batch, seq):
    grid_rows = seq // GRID_W
    steps = grid_rows // NA_QROWS
    tq = NA_QROWS * GRID_W
    tp = NA_KPIECE * GRID_W
    pieces_per_batch = grid_rows // NA_KPIECE
    npieces = NA_KROWS // NA_KPIECE
    shift = (NA_WIN_ROWS // 2) // NA_KPIECE

    def piece_map(i):
        def index(b, t):
            start = jnp.clip(t * (NA_QROWS // NA_KPIECE) - shift, 0, pieces_per_batch - npieces)
            return (b * pieces_per_batch + start + i, 0)
        return index

    def tbl_map(b, t):
        variant = jnp.where(t == 0, 0, jnp.where(t == steps - 1, 2, 1))
        return (variant, 0, 0, 0, 0)

    piece_specs = [pl.BlockSpec((tp, A_WIDTH), piece_map(i)) for i in range(npieces)]
    return pl.pallas_call(
        _na_kernel,
        grid=(batch, steps),
        in_specs=[pl.BlockSpec((tq, A_WIDTH), lambda b, t: (b * steps + t, 0))] + piece_specs + piece_specs + [
            pl.BlockSpec((1,) + tbl.shape[1:], tbl_map)],
        out_specs=pl.BlockSpec((tq, A_WIDTH), lambda b, t: (b * steps + t, 0)),
        out_shape=jax.ShapeDtypeStruct(q.shape, BF16),
        compiler_params=_params("parallel", "arbitrary"),
        name="na_attention",
    )(q, k, k, k, k, v, v, v, v, tbl)


def _mlp_kernel(*refs, n_attn, final):
    x_ref = refs[0]
    attn_refs = refs[1:1 + n_attn]
    wo_ref, g_ref, wup_ref, wdown_ref, gf_ref, o_ref, h_ref, acc_ref = refs[1 + n_attn:]
    x = x_ref[...]
    width = wo_ref.shape[0] // n_attn
    for i, a_ref in enumerate(attn_refs):
        x = x + _dot(a_ref[...], wo_ref[i * width:(i + 1) * width, :])
    h = _rmsnorm_rows(x, g_ref[...], NORM_EPS).astype(BF16)
    h_ref[...] = h

    def hidden(hh, j):
        u = jnp.maximum(_dot(hh, wup_ref[j]), 0.0)
        return _dot((u * u).astype(BF16), wdown_ref[j])

    acc_ref[...] = x + hidden(h, 0)

    def hidden_chunk(j, carry):
        acc_ref[...] += hidden(h_ref[...], j)
        return carry

    last = wup_ref.shape[0] - 1
    lax.fori_loop(1, last, hidden_chunk, 0)
    y = acc_ref[...] + hidden(h_ref[...], last)
    if final:
        y = _rmsnorm_rows(y, gf_ref[...], NORM_EPS)
    o_ref[...] = y


def _outproj_mlp(x, attn, wo, g, wup, wdown, gf, final):
    n = x.shape[0]
    tm = min(TM_MLP, n)
    row = lambda i: (i, 0)
    const2 = lambda i: (0, 0)
    const3 = lambda i: (0, 0, 0)
    return pl.pallas_call(
        functools.partial(_mlp_kernel, n_attn=len(attn), final=final),
        grid=(n // tm,),
        in_specs=[pl.BlockSpec((tm, D_MODEL), row)]
        + [pl.BlockSpec((tm, a.shape[1]), row) for a in attn]
        + [
            pl.BlockSpec(wo.shape, const2),
            pl.BlockSpec((1, D_MODEL), const2),
            pl.BlockSpec(wup.shape, const3),
            pl.BlockSpec(wdown.shape, const3),
            pl.BlockSpec((1, D_MODEL), const2),
        ],
        out_specs=pl.BlockSpec((tm, D_MODEL), row),
        out_shape=jax.ShapeDtypeStruct(x.shape, F32),
        scratch_shapes=[pltpu.VMEM((tm, D_MODEL), BF16), pltpu.VMEM((tm, D_MODEL), F32)],
        compiler_params=_params("parallel"),
        name="outproj_mlp",
    )(x, *attn, wo, g, wup, wdown, gf)


def _rope_angles(pos, dim, theta):
    inv_freq = 1.0 / jnp.power(theta, jnp.arange(0, dim, 2, dtype=F32) / dim)
    ang = pos.astype(F32)[:, None] * inv_freq[None, :]
    return jnp.cos(ang), jnp.sin(ang)


def _axial_tables(seq):
    t = jnp.arange(seq)
    half = HEAD_DIM // 2
    cr, sr = _rope_angles(t // GRID_W, half, AXIAL_THETA)
    cc, sc = _rope_angles(t % GRID_W, half, AXIAL_THETA)
    cos = jnp.concatenate([cr, cr, cc, cc], axis=-1)
    sin = jnp.concatenate([-sr, sr, -sc, sc], axis=-1)
    return jnp.tile(cos, (1, LANES // HEAD_DIM)), jnp.tile(sin, (1, LANES // HEAD_DIM))


def _rope_tables(seq):
    c, s = _rope_angles(jnp.arange(seq), HEAD_DIM, ROPE_THETA)
    cos = jnp.concatenate([c, c], axis=-1)
    sin = jnp.concatenate([-s, s], axis=-1)
    return jnp.tile(cos, (1, LANES // HEAD_DIM)), jnp.tile(sin, (1, LANES // HEAD_DIM))


_GQA_HEAD_ORDER = np.arange(GQA_Q_HEADS).reshape(GQA_KV_HEADS, -1).T.reshape(-1)


def _prepare_even(w_in, q_norm, k_norm, w_out):
    a, bq = A_WIDTH, B_Q_WIDTH
    wqb = w_in[:, 3 * a:3 * a + bq].reshape(w_in.shape[0], GQA_Q_HEADS, HEAD_DIM)[:, _GQA_HEAD_ORDER, :]
    w = jnp.concatenate([w_in[:, :3 * a], wqb.reshape(w_in.shape[0], bq), w_in[:, 3 * a + bq:]], axis=1).astype(BF16)
    reps = LANES // HEAD_DIM
    gq = jnp.tile(q_norm, reps)[None]
    gk = jnp.tile(k_norm, reps)[None]
    wo_b = w_out[a:].reshape(GQA_Q_HEADS, HEAD_DIM, -1)[_GQA_HEAD_ORDER].reshape(bq, -1)
    wo = jnp.concatenate([w_out[:a], wo_b], axis=0).astype(BF16)
    return w, gq, gk, wo


def _group_ones():
    g = np.arange(LANES) // HEAD_DIM
    return jnp.asarray((g[:, None] == g[None, :]).astype(np.float32), dtype=BF16)


def _trunk(x3, p):
    batch, seq, _ = x3.shape
    assert seq % (NA_KROWS * GRID_W) == 0
    x = x3.reshape(batch * seq, D_MODEL)
    depth = len(p["layers"])
    for layer, lp in enumerate(p["layers"]):
        final = layer == depth - 1
        if layer % 2 == 0:
            cos, sin = _axial_tables(seq)
            qa, ka, va, qb, kb, vb = _inproj_even(x, lp["ln_mix"], lp["w_in"], p["gmat"], lp["gq"], lp["gk"],
                                                  cos, sin, seq)
            a_out = _na_attention(qa, ka, va, lp["na_bias"], batch, seq)
            b_out = _gqa_attention(qb, kb, vb, batch, seq)
            attn = [a_out, b_out]
        else:
            cos, sin = _rope_tables(seq)
            q, k, v = _inproj_odd(x, lp["ln_mix"], lp["w_in"], cos, sin, seq)
            attn = [_diff_attention(q, k, v, lp["lq1"], lp["lk1"], lp["lq2"], lp["lk2"], lp["subln"],
                                    batch, seq, lp["lam_init"])]
        x = _outproj_mlp(x, attn, lp["w_out"], lp["ln_mlp"], lp["w_up"], lp["w_down"], p["ln_f"], final)
    return x.reshape(batch, seq, D_MODEL)


def kernel(x_prompt, x_sample, ln_mix_e, w_in_e, rpb, q_norm_b, k_norm_b, w_out_e, ln_mix_o, w_in_o, lambda_q1, lambda_k1, lambda_q2, lambda_k2, subln_g, w_out_o, ln_mlp, w_up, w_down, ln_f):
    depth = ln_mlp.shape[0]
    layers = []
    for layer in range(depth):
        j = layer // 2
        lp = {
            "ln_mlp": ln_mlp[layer][None],
            "w_up": w_up[layer].astype(BF16).reshape(D_MODEL, D_FF // TF_MLP, TF_MLP).transpose(1, 0, 2),
            "w_down": w_down[layer].astype(BF16).reshape(D_FF // TF_MLP, TF_MLP, D_MODEL),
        }
        if layer % 2 == 0:
            w, gq, gk, wo = _prepare_even(w_in_e[j], q_norm_b[j], k_norm_b[j], w_out_e[j])
            lp.update(ln_mix=ln_mix_e[j][None], w_in=w, gq=gq, gk=gk, w_out=wo,
                      na_bias=_na_bias_table(rpb[j]))
        else:
            lp.update(ln_mix=ln_mix_o[j][None], w_in=w_in_o[j].astype(BF16), w_out=w_out_o[j].astype(BF16),
                      lq1=lambda_q1[j][None], lk1=lambda_k1[j][None], lq2=lambda_q2[j][None],
                      lk2=lambda_k2[j][None], subln=subln_g[j][None],
                      lam_init=0.8 - 0.6 * math.exp(-0.3 * layer))
        layers.append(lp)
    p = {"layers": layers, "gmat": _group_ones(), "ln_f": ln_f[None]}
    return (_trunk(x_prompt, p), _trunk(x_sample, p))
```

```python
import functools
import math

import numpy as np
import jax
import jax.numpy as jnp
from jax import lax
from jax.experimental import pallas as pl
from jax.experimental.pallas import tpu as pltpu

F32 = jnp.float32
BF16 = jnp.bfloat16

D_MODEL = 1024
HEAD_DIM = 64
GRID_W = 64
NA_HEADS = 8
NA_WIN_ROWS = 8
NA_WIN_COLS = 16
NA_QCOLS = 16
NA_KCOLS = 32
GQA_Q_HEADS = 8
GQA_KV_HEADS = 2
AXIAL_THETA = 10000.0
DIFF_HEADS = 8
D_FF = 4 * D_MODEL
ROPE_THETA = 10000.0
NORM_EPS = 1e-6
QK_NORM_EPS = 1e-6
SUBLN_EPS = 1e-5
A_WIDTH = NA_HEADS * HEAD_DIM
B_Q_WIDTH = GQA_Q_HEADS * HEAD_DIM
B_KV_WIDTH = GQA_KV_HEADS * HEAD_DIM
DIFF_WIDTH = 2 * DIFF_HEADS * HEAD_DIM
SM_SCALE = HEAD_DIM ** -0.5
LOG2E = math.log2(math.e)
Q_SCALE = SM_SCALE * LOG2E

LANES = 128
VMEM_LIMIT = 56 * 1024 * 1024

TM_PROJ = 1024
TM_MLP = 1024
TF_MLP = 1024
SCORE_SCRATCH_BYTES = 32 * 1024 * 1024
ROWS_PV = 1024
NA_QROWS = 8
NA_KROWS = 16
NA_KPIECE = 4


def _params(*sem):
    return pltpu.CompilerParams(dimension_semantics=sem, vmem_limit_bytes=VMEM_LIMIT)


def _rmsnorm_rows(x, g, eps):
    ms = jnp.mean(x * x, axis=-1, keepdims=True)
    return x * lax.rsqrt(ms + eps) * g


def _dot(a, b):
    return jnp.dot(a, b, preferred_element_type=F32)


def _dot_nt(a, b):
    return lax.dot_general(a, b, (((1,), (1,)), ((), ())), preferred_element_type=F32)


def _lane_lo_mask():
    return lax.broadcasted_iota(jnp.int32, (1, LANES), 1) < HEAD_DIM


def _swap_halves(x, span):
    first = lax.broadcasted_iota(jnp.int32, (1, LANES), 1) % (2 * span) < span
    return jnp.where(first, pltpu.roll(x, LANES - span, 1), pltpu.roll(x, span, 1))


def _group_sumsq(x, gmat_ref):
    sq = x * x
    hi = sq.astype(BF16)
    lo = (sq - hi.astype(F32)).astype(BF16)
    return _dot(hi, gmat_ref[...]) + _dot(lo, gmat_ref[...])


def _inproj_even_kernel(x_ref, g_ref, w_ref, gmat_ref, gq_ref, gk_ref, cos_ref, sin_ref,
                        qa_ref, ka_ref, va_ref, qb_ref, kb_ref, vb_ref):
    h = _rmsnorm_rows(x_ref[...], g_ref[...], NORM_EPS).astype(BF16)
    a, bq, bkv = A_WIDTH, B_Q_WIDTH, B_KV_WIDTH
    qa_ref[...] = (_dot(h, w_ref[:, 0:a]) * Q_SCALE).astype(BF16)
    ka_ref[...] = _dot(h, w_ref[:, a:2 * a])
    va_ref[...] = _dot(h, w_ref[:, 2 * a:3 * a])
    o = 3 * a
    q = _dot(h, w_ref[:, o:o + bq])
    cos = cos_ref[...]
    sin = sin_ref[...]

    def norm_rope(x, g):
        r = lax.rsqrt(_group_sumsq(x, gmat_ref) * (1.0 / HEAD_DIM) + QK_NORM_EPS)
        xg = x * g
        return r * (xg * cos + _swap_halves(xg, HEAD_DIM // 4) * sin)

    for c in range(bq // LANES):
        sl = slice(c * LANES, (c + 1) * LANES)
        qb_ref[:, sl] = (norm_rope(q[:, sl], gq_ref[...]) * Q_SCALE).astype(BF16)
    o += bq
    kb_ref[...] = norm_rope(_dot(h, w_ref[:, o:o + bkv]), gk_ref[...]).astype(BF16)
    o += bkv
    vb_ref[...] = _dot(h, w_ref[:, o:o + bkv]).astype(BF16)


def _inproj_even(x, g, w, gmat, gq, gk, cos, sin, seq):
    n = x.shape[0]
    tm = min(TM_PROJ, seq)
    pos_blocks = seq // tm
    row = lambda i: (i, 0)
    const = lambda i: (0, 0)
    pos = lambda i: (i % pos_blocks, 0)
    wcols = w.shape[1]
    return pl.pallas_call(
        _inproj_even_kernel,
        grid=(n // tm,),
        in_specs=[
            pl.BlockSpec((tm, D_MODEL), row),
            pl.BlockSpec((1, D_MODEL), const),
            pl.BlockSpec((D_MODEL, wcols), const),
            pl.BlockSpec((LANES, LANES), const),
            pl.BlockSpec((1, LANES), const),
            pl.BlockSpec((1, LANES), const),
            pl.BlockSpec((tm, LANES), pos),
            pl.BlockSpec((tm, LANES), pos),
        ],
        out_specs=[
            pl.BlockSpec((tm, A_WIDTH), row),
            pl.BlockSpec((tm, A_WIDTH), row),
            pl.BlockSpec((tm, A_WIDTH), row),
            pl.BlockSpec((tm, B_Q_WIDTH), row),
            pl.BlockSpec((tm, B_KV_WIDTH), row),
            pl.BlockSpec((tm, B_KV_WIDTH), row),
        ],
        out_shape=[
            jax.ShapeDtypeStruct((n, A_WIDTH), BF16),
            jax.ShapeDtypeStruct((n, A_WIDTH), F32),
            jax.ShapeDtypeStruct((n, A_WIDTH), F32),
            jax.ShapeDtypeStruct((n, B_Q_WIDTH), BF16),
            jax.ShapeDtypeStruct((n, B_KV_WIDTH), BF16),
            jax.ShapeDtypeStruct((n, B_KV_WIDTH), BF16),
        ],
        compiler_params=_params("parallel"),
        name="inproj_even",
    )(x, g, w, gmat, gq, gk, cos, sin)


def _inproj_odd_kernel(x_ref, g_ref, w_ref, cos_ref, sin_ref, q_ref, k_ref, v_ref):
    h = _rmsnorm_rows(x_ref[...], g_ref[...], NORM_EPS).astype(BF16)
    d = DIFF_WIDTH
    cos = cos_ref[...]
    sin = sin_ref[...]

    def rope(o, scale, out_ref):
        x = _dot(h, w_ref[:, o:o + d])
        for c in range(d // LANES):
            xc = x[:, c * LANES:(c + 1) * LANES]
            out = xc * cos + _swap_halves(xc, HEAD_DIM // 2) * sin
            out_ref[:, c * LANES:(c + 1) * LANES] = (out * scale).astype(BF16)

    rope(0, Q_SCALE, q_ref)
    rope(d, 1.0, k_ref)
    v_ref[...] = _dot(h, w_ref[:, 2 * d:3 * d]).astype(BF16)


def _inproj_odd(x, g, w, cos, sin, seq):
    n = x.shape[0]
    tm = min(TM_PROJ, seq)
    pos_blocks = seq // tm
    row = lambda i: (i, 0)
    const = lambda i: (0, 0)
    pos = lambda i: (i % pos_blocks, 0)
    out = jax.ShapeDtypeStruct((n, DIFF_WIDTH), BF16)
    return pl.pallas_call(
        _inproj_odd_kernel,
        grid=(n // tm,),
        in_specs=[
            pl.BlockSpec((tm, D_MODEL), row),
            pl.BlockSpec((1, D_MODEL), const),
            pl.BlockSpec((D_MODEL, w.shape[1]), const),
            pl.BlockSpec((tm, LANES), pos),
            pl.BlockSpec((tm, LANES), pos),
        ],
        out_specs=[pl.BlockSpec((tm, DIFF_WIDTH), row)] * 3,
        out_shape=[out, out, out],
        compiler_params=_params("parallel"),
        name="inproj_odd",
    )(x, g, w, cos, sin)


def _stack_masked_heads(q_ref, qm_ref, ncols, tq, half):
    lo = _lane_lo_mask()
    pieces = tq // half
    for c in range(ncols):
        for j in range(pieces):
            qc = q_ref[j * half:(j + 1) * half, c * LANES:(c + 1) * LANES]
            zero = jnp.zeros_like(qc)
            base = (c * pieces + j) * 2 * half
            qm_ref[base:base + half, :] = jnp.where(lo, qc, zero)
            qm_ref[base + half:base + 2 * half, :] = jnp.where(lo, zero, qc)


def _softmax_pv(first_tile, qm_ref, k_ref, v_ref, s_ref, m_ref, vext_ref, acc_ref):
    seq = k_ref.shape[0]
    rb = min(ROWS_PV, qm_ref.shape[0])

    @pl.when(first_tile)
    def _():
        vext_ref[:, 0:LANES] = v_ref[...]
        vext_ref[:, LANES:2 * LANES] = jnp.ones(v_ref.shape, BF16)

    s = _dot_nt(qm_ref[...], k_ref[...])
    s_ref[...] = s
    m = s[:, 0:LANES]
    for i in range(1, seq // LANES):
        m = jnp.maximum(m, s[:, i * LANES:(i + 1) * LANES])
    m_ref[...] = jnp.broadcast_to(jnp.max(m, axis=-1, keepdims=True), m_ref.shape)

    def values(rows):
        s = s_ref[rows, :]
        m = m_ref[rows, :]
        ps = [jnp.exp2(s[:, i * LANES:(i + 1) * LANES] - m).astype(BF16) for i in range(seq // LANES)]
        acc_ref[rows, :] = _dot(jnp.concatenate(ps, axis=1), vext_ref[...])

    for r in range(qm_ref.shape[0] // rb):
        values(slice(r * rb, (r + 1) * rb))


def _attn_rows(seq):
    return SCORE_SCRATCH_BYTES // (4 * seq)


def _attn_scratch(rows, seq):
    return [
        pltpu.VMEM((rows, LANES), BF16),
        pltpu.VMEM((rows, seq), F32),
        pltpu.VMEM((rows, LANES), F32),
        pltpu.VMEM((seq, 2 * LANES), BF16),
        pltpu.VMEM((rows, 2 * LANES), F32),
    ]


def _gqa_kernel(q_ref, k_ref, v_ref, o_ref, qm_ref, s_ref, m_ref, vext_ref, acc_ref, *, tq):
    ncols = B_Q_WIDTH // LANES
    _stack_masked_heads(q_ref, qm_ref, ncols, tq, tq)
    _softmax_pv(pl.program_id(1) == 0, qm_ref, k_ref, v_ref, s_ref, m_ref, vext_ref, acc_ref)
    lo = _lane_lo_mask()
    for c in range(ncols):
        a_lo = acc_ref[2 * c * tq:(2 * c + 1) * tq, :]
        a_hi = acc_ref[(2 * c + 1) * tq:(2 * c + 2) * tq, :]
        num = jnp.where(lo, a_lo[:, 0:LANES], a_hi[:, 0:LANES])
        den = jnp.where(lo, a_lo[:, LANES:2 * LANES], a_hi[:, LANES:2 * LANES])
        o_ref[:, c * LANES:(c + 1) * LANES] = (num / den).astype(BF16)


def _gqa_attention(q, k, v, batch, seq):
    ncols = B_Q_WIDTH // LANES
    tq = min(_attn_rows(seq) // (2 * ncols), seq)
    rows = 2 * ncols * tq
    nq = seq // tq
    return pl.pallas_call(
        functools.partial(_gqa_kernel, tq=tq),
        grid=(batch, nq),
        in_specs=[
            pl.BlockSpec((tq, B_Q_WIDTH), lambda b, i: (b * nq + i, 0)),
            pl.BlockSpec((seq, B_KV_WIDTH), lambda b, i: (b, 0)),
            pl.BlockSpec((seq, B_KV_WIDTH), lambda b, i: (b, 0)),
        ],
        out_specs=pl.BlockSpec((tq, B_Q_WIDTH), lambda b, i: (b * nq + i, 0)),
        out_shape=jax.ShapeDtypeStruct(q.shape, BF16),
        scratch_shapes=_attn_scratch(rows, seq),
        compiler_params=_params("parallel", "arbitrary"),
        name="gqa_attention",
    )(q, k, v)


def _diff_kernel(q_ref, k_ref, v_ref, lq1_ref, lk1_ref, lq2_ref, lk2_ref, g_ref, o_ref,
                 qm_ref, s_ref, m_ref, vext_ref, acc_ref, *, tq, lam_init):
    half = min(ROWS_PV // 2, tq)
    _stack_masked_heads(q_ref, qm_ref, 1, tq, half)
    _softmax_pv(pl.program_id(2) == 0, qm_ref, k_ref, v_ref, s_ref, m_ref, vext_ref, acc_ref)
    lam = (jnp.exp(jnp.sum(lq1_ref[...] * lk1_ref[...], axis=-1, keepdims=True))
           - jnp.exp(jnp.sum(lq2_ref[...] * lk2_ref[...], axis=-1, keepdims=True)) + lam_init)
    for j in range(tq // half):
        a = acc_ref[2 * j * half:(2 * j + 2) * half, :]
        o = a[:, 0:LANES] / a[:, LANES:2 * LANES]
        d = o[0:half] - lam * o[half:2 * half]
        o_ref[j * half:(j + 1) * half, :] = (_rmsnorm_rows(d, g_ref[...], SUBLN_EPS) * (1.0 - lam_init)).astype(BF16)


def _diff_attention(q, k, v, lq1, lk1, lq2, lk2, g, batch, seq, lam_init):
    tq = min(_attn_rows(seq) // 2, seq)
    rows = 2 * tq
    nq = seq // tq
    vec = pl.BlockSpec((1, HEAD_DIM), lambda b, h, i: (0, 0))
    return pl.pallas_call(
        functools.partial(_diff_kernel, tq=tq, lam_init=lam_init),
        grid=(batch, DIFF_HEADS, nq),
        in_specs=[
            pl.BlockSpec((tq, LANES), lambda b, h, i: (b * nq + i, h)),
            pl.BlockSpec((seq, LANES), lambda b, h, i: (b, h)),
            pl.BlockSpec((seq, LANES), lambda b, h, i: (b, h)),
            vec, vec, vec, vec,
            pl.BlockSpec((1, LANES), lambda b, h, i: (0, 0)),
        ],
        out_specs=pl.BlockSpec((tq, LANES), lambda b, h, i: (b * nq + i, h)),
        out_shape=jax.ShapeDtypeStruct(q.shape, BF16),
        scratch_shapes=_attn_scratch(rows, seq),
        compiler_params=_params("parallel", "parallel", "arbitrary"),
        name="diff_attention",
    )(q, k, v, lq1, lk1, lq2, lk2, g)


def _na_band_start(n):
    return int(np.clip(n * NA_QCOLS - NA_WIN_COLS // 2, 0, GRID_W - NA_KCOLS))


def _na_bias_indices():
    qr = np.arange(NA_QROWS)[:, None]
    kr = np.arange(NA_KROWS)[None, :]
    half = NA_WIN_ROWS // 2
    key_shift = [0, -half, -(NA_KROWS - NA_QROWS)]
    win_start = [np.maximum(qr - half, 0), qr, np.minimum(qr + half, NA_KROWS - NA_WIN_ROWS)]
    drow = np.stack([np.clip(kr + shift - qr + NA_WIN_ROWS - 1, 0, 2 * NA_WIN_ROWS - 2) for shift in key_shift])
    rmask = np.stack([(kr >= w0) & (kr < w0 + NA_WIN_ROWS) for w0 in win_start])
    qc = np.arange(NA_QCOLS)[:, None]
    kc = np.arange(NA_KCOLS)[None, :]
    dcol, cmask = [], []
    for n in range(GRID_W // NA_QCOLS):
        qabs = n * NA_QCOLS + qc
        kabs = _na_band_start(n) + kc
        w0 = np.clip(qabs - NA_WIN_COLS // 2, 0, GRID_W - NA_WIN_COLS)
        dcol.append(np.clip(kabs - qabs + NA_WIN_COLS - 1, 0, 2 * NA_WIN_COLS - 2))
        cmask.append((kabs >= w0) & (kabs < w0 + NA_WIN_COLS))
    return drow, rmask, np.stack(dcol), np.stack(cmask)


def _na_bias_table(rpb):
    drow, rmask, dcol, cmask = _na_bias_indices()
    oh_r = jnp.asarray(np.eye(2 * NA_WIN_ROWS - 1, dtype=np.float32)[drow])
    oh_c = jnp.asarray(np.eye(2 * NA_WIN_COLS - 1, dtype=np.float32)[dcol])
    hp = lax.Precision.HIGHEST
    a = jnp.einsum("hab,vqka->hvqkb", rpb.astype(F32) * LOG2E, oh_r, precision=hp)
    t = jnp.einsum("hvqkb,nxyb->vnhqxky", a, oh_c, precision=hp)
    t = t.reshape(t.shape[:3] + (NA_QROWS * NA_QCOLS, NA_KROWS * NA_KCOLS))
    mask = rmask[:, None, :, None, :, None] & cmask[None, :, None, :, None, :]
    mask = mask.reshape(mask.shape[:2] + (1,) + t.shape[3:])
    return jnp.where(jnp.asarray(mask), t, -jnp.inf)


def _na_kernel(q_ref, k0, k1, k2, k3, v0, v1, v2, v3, tbl_ref, o_ref):
    kps = (k0, k1, k2, k3)
    vps = (v0, v1, v2, v3)
    lo = _lane_lo_mask()
    nq = NA_QROWS * NA_QCOLS
    for n in range(GRID_W // NA_QCOLS):
        band = _na_band_start(n)
        for c in range(A_WIDTH // LANES):
            lanes = slice(c * LANES, (c + 1) * LANES)
            qs = jnp.concatenate(
                [q_ref[r * GRID_W + n * NA_QCOLS:r * GRID_W + (n + 1) * NA_QCOLS, lanes] for r in range(NA_QROWS)],
                axis=0)
            zero = jnp.zeros_like(qs)
            qm = jnp.concatenate([jnp.where(lo, qs, zero), jnp.where(lo, zero, qs)], axis=0)

            def band_rows(pieces):
                return jnp.concatenate(
                    [p[r * GRID_W + band:r * GRID_W + band + NA_KCOLS, lanes]
                     for p in pieces for r in range(NA_KPIECE)], axis=0).astype(BF16)

            kb = band_rows(kps)
            vb = band_rows(vps)
            s = _dot_nt(qm, kb)
            s = s + jnp.concatenate([tbl_ref[0, n, 2 * c], tbl_ref[0, n, 2 * c + 1]], axis=0)
            m = jnp.max(s, axis=-1, keepdims=True)
            p = jnp.exp2(s - m)
            l = jnp.sum(p, axis=-1, keepdims=True)
            o = _dot(p.astype(BF16), vb) / l
            res = jnp.where(lo, o[0:nq], o[nq:2 * nq]).astype(BF16)
            for r in range(NA_QROWS):
                o_ref[r * GRID_W + n * NA_QCOLS:r * GRID_W + (n + 1) * NA_QCOLS, lanes] = (
                    res[r * NA_QCOLS:(r + 1) * NA_QCOLS])


def _na_attention(q, k, v, tbl, batch, seq):
    grid_rows = seq // GRID_W
    steps = grid_rows // NA_QROWS
    tq = NA_QROWS * GRID_W
    tp = NA_KPIECE * GRID_W
    pieces_per_batch = grid_rows // NA_KPIECE
    npieces = NA_KROWS // NA_KPIECE
    shift = (NA_WIN_ROWS // 2) // NA_KPIECE

    def piece_map(i):
        def index(b, t):
            start = jnp.clip(t * (NA_QROWS // NA_KPIECE) - shift, 0, pieces_per_batch - npieces)
            return (b * pieces_per_batch + start + i, 0)
        return index

    def tbl_map(b, t):
        variant = jnp.where(t == 0, 0, jnp.where(t == steps - 1, 2, 1))
        return (variant, 0, 0, 0, 0)

    piece_specs = [pl.BlockSpec((tp, A_WIDTH), piece_map(i)) for i in range(npieces)]
    return pl.pallas_call(
        _na_kernel,
        grid=(batch, steps),
        in_specs=[pl.BlockSpec((tq, A_WIDTH), lambda b, t: (b * steps + t, 0))] + piece_specs + piece_specs + [
            pl.BlockSpec((1,) + tbl.shape[1:], tbl_map)],
        out_specs=pl.BlockSpec((tq, A_WIDTH), lambda b, t: (b * steps + t, 0)),
        out_shape=jax.ShapeDtypeStruct(q.shape, BF16),
        compiler_params=_params("parallel", "arbitrary"),
        name="na_attention",
    )(q, k, k, k, k, v, v, v, v, tbl)


def _mlp_kernel(*refs, n_attn, final):
    x_ref = refs[0]
    attn_refs = refs[1:1 + n_attn]
    wo_ref, g_ref, wup_ref, wdown_ref, gf_ref, o_ref, h_ref, acc_ref = refs[1 + n_attn:]
    x = x_ref[...]
    width = wo_ref.shape[0] // n_attn
    for i, a_ref in enumerate(attn_refs):
        x = x + _dot(a_ref[...], wo_ref[i * width:(i + 1) * width, :])
    h = _rmsnorm_rows(x, g_ref[...], NORM_EPS).astype(BF16)
    h_ref[...] = h

    def hidden(hh, j):
        u = jnp.maximum(_dot(hh, wup_ref[j]), 0.0)
        return _dot((u * u).astype(BF16), wdown_ref[j])

    acc_ref[...] = x + hidden(h, 0)

    def hidden_chunk(j, carry):
        acc_ref[...] += hidden(h_ref[...], j)
        return carry

    last = wup_ref.shape[0] - 1
    lax.fori_loop(1, last, hidden_chunk, 0)
    y = acc_ref[...] + hidden(h_ref[...], last)
    if final:
        y = _rmsnorm_rows(y, gf_ref[...], NORM_EPS)
    o_ref[...] = y


def _outproj_mlp(x, attn, wo, g, wup, wdown, gf, final):
    n = x.shape[0]
    tm = min(TM_MLP, n)
    row = lambda i: (i, 0)
    const2 = lambda i: (0, 0)
    const3 = lambda i: (0, 0, 0)
    return pl.pallas_call(
        functools.partial(_mlp_kernel, n_attn=len(attn), final=final),
        grid=(n // tm,),
        in_specs=[pl.BlockSpec((tm, D_MODEL), row)]
        + [pl.BlockSpec((tm, a.shape[1]), row) for a in attn]
        + [
            pl.BlockSpec(wo.shape, const2),
            pl.BlockSpec((1, D_MODEL), const2),
            pl.BlockSpec(wup.shape, const3),
            pl.BlockSpec(wdown.shape, const3),
            pl.BlockSpec((1, D_MODEL), const2),
        ],
        out_specs=pl.BlockSpec((tm, D_MODEL), row),
        out_shape=jax.ShapeDtypeStruct(x.shape, F32),
        scratch_shapes=[pltpu.VMEM((tm, D_MODEL), BF16), pltpu.VMEM((tm, D_MODEL), F32)],
        compiler_params=_params("parallel"),
        name="outproj_mlp",
    )(x, *attn, wo, g, wup, wdown, gf)


def _rope_angles(pos, dim, theta):
    inv_freq = 1.0 / jnp.power(theta, jnp.arange(0, dim, 2, dtype=F32) / dim)
    ang = pos.astype(F32)[:, None] * inv_freq[None, :]
    return jnp.cos(ang), jnp.sin(ang)


def _axial_tables(seq):
    t = jnp.arange(seq)
    half = HEAD_DIM // 2
    cr, sr = _rope_angles(t // GRID_W, half, AXIAL_THETA)
    cc, sc = _rope_angles(t % GRID_W, half, AXIAL_THETA)
    cos = jnp.concatenate([cr, cr, cc, cc], axis=-1)
    sin = jnp.concatenate([-sr, sr, -sc, sc], axis=-1)
    return jnp.tile(cos, (1, LANES // HEAD_DIM)), jnp.tile(sin, (1, LANES // HEAD_DIM))


def _rope_tables(seq):
    c, s = _rope_angles(jnp.arange(seq), HEAD_DIM, ROPE_THETA)
    cos = jnp.concatenate([c, c], axis=-1)
    sin = jnp.concatenate([-s, s], axis=-1)
    return jnp.tile(cos, (1, LANES // HEAD_DIM)), jnp.tile(sin, (1, LANES // HEAD_DIM))


_GQA_HEAD_ORDER = np.arange(GQA_Q_HEADS).reshape(GQA_KV_HEADS, -1).T.reshape(-1)


def _prepare_even(w_in, q_norm, k_norm, w_out):
    a, bq = A_WIDTH, B_Q_WIDTH
    wqb = w_in[:, 3 * a:3 * a + bq].reshape(w_in.shape[0], GQA_Q_HEADS, HEAD_DIM)[:, _GQA_HEAD_ORDER, :]
    w = jnp.concatenate([w_in[:, :3 * a], wqb.reshape(w_in.shape[0], bq), w_in[:, 3 * a + bq:]], axis=1).astype(BF16)
    reps = LANES // HEAD_DIM
    gq = jnp.tile(q_norm, reps)[None]
    gk = jnp.tile(k_norm, reps)[None]
    wo_b = w_out[a:].reshape(GQA_Q_HEADS, HEAD_DIM, -1)[_GQA_HEAD_ORDER].reshape(bq, -1)
    wo = jnp.concatenate([w_out[:a], wo_b], axis=0).astype(BF16)
    return w, gq, gk, wo


def _group_ones():
    g = np.arange(LANES) // HEAD_DIM
    return jnp.asarray((g[:, None] == g[None, :]).astype(np.float32), dtype=BF16)


def _trunk(x3, p):
    batch, seq, _ = x3.shape
    assert seq % (NA_KROWS * GRID_W) == 0
    x = x3.reshape(batch * seq, D_MODEL)
    depth = len(p["layers"])
    for layer, lp in enumerate(p["layers"]):
        final = layer == depth - 1
        if layer % 2 == 0:
            cos, sin = _axial_tables(seq)
            qa, ka, va, qb, kb, vb = _inproj_even(x, lp["ln_mix"], lp["w_in"], p["gmat"], lp["gq"], lp["gk"],
                                                  cos, sin, seq)
            a_out = _na_attention(qa, ka, va, lp["na_bias"], batch, seq)
            b_out = _gqa_attention(qb, kb, vb, batch, seq)
            attn = [a_out, b_out]
        else:
            cos, sin = _rope_tables(seq)
            q, k, v = _inproj_odd(x, lp["ln_mix"], lp["w_in"], cos, sin, seq)
            attn = [_diff_attention(q, k, v, lp["lq1"], lp["lk1"], lp["lq2"], lp["lk2"], lp["subln"],
                                    batch, seq, lp["lam_init"])]
        x = _outproj_mlp(x, attn, lp["w_out"], lp["ln_mlp"], lp["w_up"], lp["w_down"], p["ln_f"], final)
    return x.reshape(batch, seq, D_MODEL)


def kernel(x_prompt, x_sample, ln_mix_e, w_in_e, rpb, q_norm_b, k_norm_b, w_out_e, ln_mix_o, w_in_o, lambda_q1, lambda_k1, lambda_q2, lambda_k2, subln_g, w_out_o, ln_mlp, w_up, w_down, ln_f):
    depth = ln_mlp.shape[0]
    layers = []
    for layer in range(depth):
        j = layer // 2
        lp = {
            "ln_mlp": ln_mlp[layer][None],
            "w_up": w_up[layer].astype(BF16).reshape(D_MODEL, D_FF // TF_MLP, TF_MLP).transpose(1, 0, 2),
            "w_down": w_down[layer].astype(BF16).reshape(D_FF // TF_MLP, TF_MLP, D_MODEL),
        }
        if layer % 2 == 0:
            w, gq, gk, wo = _prepare_even(w_in_e[j], q_norm_b[j], k_norm_b[j], w_out_e[j])
            lp.update(ln_mix=ln_mix_e[j][None], w_in=w, gq=gq, gk=gk, w_out=wo,
                      na_bias=_na_bias_table(rpb[j]))
        else:
            lp.update(ln_mix=ln_mix_o[j][None], w_in=w_in_o[j].astype(BF16), w_out=w_out_o[j].astype(BF16),
                      lq1=lambda_q1[j][None], lk1=lambda_k1[j][None], lq2=lambda_q2[j][None],
                      lk2=lambda_k2[j][None], subln=subln_g[j][None],
                      lam_init=0.8 - 0.6 * math.exp(-0.3 * layer))
        layers.append(lp)
    p = {"layers": layers, "gmat": _group_ones(), "ln_f": ln_f[None]}
    return (_trunk(x_prompt, p), _trunk(x_sample, p))
```

```python
import functools
import math

import numpy as np
import jax
import jax.numpy as jnp
from jax import lax
from jax.experimental import pallas as pl
from jax.experimental.pallas import tpu as pltpu

F32 = jnp.float32
BF16 = jnp.bfloat16

D_MODEL = 1024
HEAD_DIM = 64
GRID_W = 64
NA_HEADS = 8
NA_WIN_ROWS = 8
NA_WIN_COLS = 16
NA_QCOLS = 16
NA_KCOLS = 32
GQA_Q_HEADS = 8
GQA_KV_HEADS = 2
AXIAL_THETA = 10000.0
DIFF_HEADS = 8
D_FF = 4 * D_MODEL
ROPE_THETA = 10000.0
NORM_EPS = 1e-6
QK_NORM_EPS = 1e-6
SUBLN_EPS = 1e-5
A_WIDTH = NA_HEADS * HEAD_DIM
B_Q_WIDTH = GQA_Q_HEADS * HEAD_DIM
B_KV_WIDTH = GQA_KV_HEADS * HEAD_DIM
DIFF_WIDTH = 2 * DIFF_HEADS * HEAD_DIM
SM_SCALE = HEAD_DIM ** -0.5
LOG2E = math.log2(math.e)
Q_SCALE = SM_SCALE * LOG2E

LANES = 128
VMEM_LIMIT = 56 * 1024 * 1024

TM_PROJ = 1024
TM_MLP = 1024
TF_MLP = 1024
SCORE_SCRATCH_BYTES = 32 * 1024 * 1024
ROWS_PV = 1024
TILES_PER_STEP = 2
NA_QROWS = 8
NA_KROWS = 16
NA_KPIECE = 4


def _params(*sem):
    return pltpu.CompilerParams(dimension_semantics=sem, vmem_limit_bytes=VMEM_LIMIT)


def _rmsnorm_rows(x, g, eps):
    ms = jnp.mean(x * x, axis=-1, keepdims=True)
    return x * lax.rsqrt(ms + eps) * g


def _dot(a, b):
    return jnp.dot(a, b, preferred_element_type=F32)


def _dot_nt(a, b):
    return lax.dot_general(a, b, (((1,), (1,)), ((), ())), preferred_element_type=F32)


def _lane_lo_mask():
    return lax.broadcasted_iota(jnp.int32, (1, LANES), 1) < HEAD_DIM


def _swap_halves(x, span):
    first = lax.broadcasted_iota(jnp.int32, (1, LANES), 1) % (2 * span) < span
    return jnp.where(first, pltpu.roll(x, LANES - span, 1), pltpu.roll(x, span, 1))


def _group_sumsq(x, gmat_ref):
    sq = x * x
    hi = sq.astype(BF16)
    lo = (sq - hi.astype(F32)).astype(BF16)
    return _dot(hi, gmat_ref[...]) + _dot(lo, gmat_ref[...])


def _inproj_even_kernel(x_ref, g_ref, w_ref, gmat_ref, gq_ref, gk_ref, cos_ref, sin_ref,
                        qa_ref, ka_ref, va_ref, qb_ref, kb_ref, vb_ref):
    h = _rmsnorm_rows(x_ref[...], g_ref[...], NORM_EPS).astype(BF16)
    a, bq, bkv = A_WIDTH, B_Q_WIDTH, B_KV_WIDTH
    qa_ref[...] = (_dot(h, w_ref[:, 0:a]) * Q_SCALE).astype(BF16)
    ka_ref[...] = _dot(h, w_ref[:, a:2 * a])
    va_ref[...] = _dot(h, w_ref[:, 2 * a:3 * a])
    o = 3 * a
    q = _dot(h, w_ref[:, o:o + bq])
    cos = cos_ref[...]
    sin = sin_ref[...]

    def norm_rope(x, g):
        r = lax.rsqrt(_group_sumsq(x, gmat_ref) * (1.0 / HEAD_DIM) + QK_NORM_EPS)
        xg = x * g
        return r * (xg * cos + _swap_halves(xg, HEAD_DIM // 4) * sin)

    for c in range(bq // LANES):
        sl = slice(c * LANES, (c + 1) * LANES)
        qb_ref[:, sl] = (norm_rope(q[:, sl], gq_ref[...]) * Q_SCALE).astype(BF16)
    o += bq
    kb_ref[...] = norm_rope(_dot(h, w_ref[:, o:o + bkv]), gk_ref[...]).astype(BF16)
    o += bkv
    vb_ref[...] = _dot(h, w_ref[:, o:o + bkv]).astype(BF16)


def _inproj_even(x, g, w, gmat, gq, gk, cos, sin, seq):
    n = x.shape[0]
    tm = min(TM_PROJ, seq)
    pos_blocks = seq // tm
    row = lambda i: (i, 0)
    const = lambda i: (0, 0)
    pos = lambda i: (i % pos_blocks, 0)
    wcols = w.shape[1]
    return pl.pallas_call(
        _inproj_even_kernel,
        grid=(n // tm,),
        in_specs=[
            pl.BlockSpec((tm, D_MODEL), row),
            pl.BlockSpec((1, D_MODEL), const),
            pl.BlockSpec((D_MODEL, wcols), const),
            pl.BlockSpec((LANES, LANES), const),
            pl.BlockSpec((1, LANES), const),
            pl.BlockSpec((1, LANES), const),
            pl.BlockSpec((tm, LANES), pos),
            pl.BlockSpec((tm, LANES), pos),
        ],
        out_specs=[
            pl.BlockSpec((tm, A_WIDTH), row),
            pl.BlockSpec((tm, A_WIDTH), row),
            pl.BlockSpec((tm, A_WIDTH), row),
            pl.BlockSpec((tm, B_Q_WIDTH), row),
            pl.BlockSpec((tm, B_KV_WIDTH), row),
            pl.BlockSpec((tm, B_KV_WIDTH), row),
        ],
        out_shape=[
            jax.ShapeDtypeStruct((n, A_WIDTH), BF16),
            jax.ShapeDtypeStruct((n, A_WIDTH), F32),
            jax.ShapeDtypeStruct((n, A_WIDTH), F32),
            jax.ShapeDtypeStruct((n, B_Q_WIDTH), BF16),
            jax.ShapeDtypeStruct((n, B_KV_WIDTH), BF16),
            jax.ShapeDtypeStruct((n, B_KV_WIDTH), BF16),
        ],
        compiler_params=_params("parallel"),
        name="inproj_even",
    )(x, g, w, gmat, gq, gk, cos, sin)


def _inproj_odd_kernel(x_ref, g_ref, w_ref, cos_ref, sin_ref, q_ref, k_ref, v_ref):
    h = _rmsnorm_rows(x_ref[...], g_ref[...], NORM_EPS).astype(BF16)
    d = DIFF_WIDTH
    cos = cos_ref[...]
    sin = sin_ref[...]

    def rope(o, scale, out_ref):
        x = _dot(h, w_ref[:, o:o + d])
        for c in range(d // LANES):
            xc = x[:, c * LANES:(c + 1) * LANES]
            out = xc * cos + _swap_halves(xc, HEAD_DIM // 2) * sin
            out_ref[:, c * LANES:(c + 1) * LANES] = (out * scale).astype(BF16)

    rope(0, Q_SCALE, q_ref)
    rope(d, 1.0, k_ref)
    v_ref[...] = _dot(h, w_ref[:, 2 * d:3 * d]).astype(BF16)


def _inproj_odd(x, g, w, cos, sin, seq):
    n = x.shape[0]
    tm = min(TM_PROJ, seq)
    pos_blocks = seq // tm
    row = lambda i: (i, 0)
    const = lambda i: (0, 0)
    pos = lambda i: (i % pos_blocks, 0)
    out = jax.ShapeDtypeStruct((n, DIFF_WIDTH), BF16)
    return pl.pallas_call(
        _inproj_odd_kernel,
        grid=(n // tm,),
        in_specs=[
            pl.BlockSpec((tm, D_MODEL), row),
            pl.BlockSpec((1, D_MODEL), const),
            pl.BlockSpec((D_MODEL, w.shape[1]), const),
            pl.BlockSpec((tm, LANES), pos),
            pl.BlockSpec((tm, LANES), pos),
        ],
        out_specs=[pl.BlockSpec((tm, DIFF_WIDTH), row)] * 3,
        out_shape=[out, out, out],
        compiler_params=_params("parallel"),
        name="inproj_odd",
    )(x, g, w, cos, sin)


def _stack_masked_heads(q_ref, qm_ref, ncols, tq, half):
    lo = _lane_lo_mask()
    pieces = tq // half
    for c in range(ncols):
        for j in range(pieces):
            qc = q_ref[j * half:(j + 1) * half, c * LANES:(c + 1) * LANES]
            zero = jnp.zeros_like(qc)
            base = (c * pieces + j) * 2 * half
            qm_ref[base:base + half, :] = jnp.where(lo, qc, zero)
            qm_ref[base + half:base + 2 * half, :] = jnp.where(lo, zero, qc)


def _softmax_pv(first_tile, qm_ref, k_ref, v_ref, s_ref, m_ref, vext_ref, acc_ref):
    seq = k_ref.shape[0]
    rb = min(ROWS_PV, qm_ref.shape[0])

    @pl.when(first_tile)
    def _():
        vext_ref[:, 0:LANES] = v_ref[...]
        vext_ref[:, LANES:2 * LANES] = jnp.ones(v_ref.shape, BF16)

    s = _dot_nt(qm_ref[...], k_ref[...])
    s_ref[...] = s
    m = s[:, 0:LANES]
    for i in range(1, seq // LANES):
        m = jnp.maximum(m, s[:, i * LANES:(i + 1) * LANES])
    m_ref[...] = jnp.broadcast_to(jnp.max(m, axis=-1, keepdims=True), m_ref.shape)

    def values(rows):
        s = s_ref[rows, :]
        m = m_ref[rows, :]
        ps = [jnp.exp2(s[:, i * LANES:(i + 1) * LANES] - m).astype(BF16) for i in range(seq // LANES)]
        acc_ref[rows, :] = _dot(jnp.concatenate(ps, axis=1), vext_ref[...])

    for r in range(qm_ref.shape[0] // rb):
        values(slice(r * rb, (r + 1) * rb))


def _attn_rows(seq):
    return SCORE_SCRATCH_BYTES // (4 * seq)


def _attn_scratch(rows, seq):
    return [
        pltpu.VMEM((rows, LANES), BF16),
        pltpu.VMEM((rows, seq), F32),
        pltpu.VMEM((rows, LANES), F32),
        pltpu.VMEM((seq, 2 * LANES), BF16),
        pltpu.VMEM((rows, 2 * LANES), F32),
    ]


def _gqa_kernel(q_ref, k_ref, v_ref, o_ref, qm_ref, s_ref, m_ref, vext_ref, acc_ref, *, tq, tiles):
    ncols = B_Q_WIDTH // LANES
    lo = _lane_lo_mask()
    for t in range(tiles):
        rows = pl.ds(t * tq, tq)
        _stack_masked_heads(q_ref.at[rows, :], qm_ref, ncols, tq, tq)
        _softmax_pv(jnp.logical_and(pl.program_id(1) == 0, t == 0), qm_ref, k_ref, v_ref, s_ref, m_ref, vext_ref,
                    acc_ref)
        for c in range(ncols):
            a_lo = acc_ref[2 * c * tq:(2 * c + 1) * tq, :]
            a_hi = acc_ref[(2 * c + 1) * tq:(2 * c + 2) * tq, :]
            num = jnp.where(lo, a_lo[:, 0:LANES], a_hi[:, 0:LANES])
            den = jnp.where(lo, a_lo[:, LANES:2 * LANES], a_hi[:, LANES:2 * LANES])
            o_ref[t * tq:(t + 1) * tq, c * LANES:(c + 1) * LANES] = (num / den).astype(BF16)


def _gqa_attention(q, k, v, batch, seq):
    ncols = B_Q_WIDTH // LANES
    tq = min(_attn_rows(seq) // (2 * ncols), seq)
    rows = 2 * ncols * tq
    tiles = min(TILES_PER_STEP, seq // tq)
    nq = seq // (tq * tiles)
    return pl.pallas_call(
        functools.partial(_gqa_kernel, tq=tq, tiles=tiles),
        grid=(batch, nq),
        in_specs=[
            pl.BlockSpec((tq * tiles, B_Q_WIDTH), lambda b, i: (b * nq + i, 0)),
            pl.BlockSpec((seq, B_KV_WIDTH), lambda b, i: (b, 0)),
            pl.BlockSpec((seq, B_KV_WIDTH), lambda b, i: (b, 0)),
        ],
        out_specs=pl.BlockSpec((tq * tiles, B_Q_WIDTH), lambda b, i: (b * nq + i, 0)),
        out_shape=jax.ShapeDtypeStruct(q.shape, BF16),
        scratch_shapes=_attn_scratch(rows, seq),
        compiler_params=_params("parallel", "arbitrary"),
        name="gqa_attention",
    )(q, k, v)


def _diff_kernel(q_ref, k_ref, v_ref, lq1_ref, lk1_ref, lq2_ref, lk2_ref, g_ref, o_ref,
                 qm_ref, s_ref, m_ref, vext_ref, acc_ref, *, tq, tiles, lam_init):
    half = min(ROWS_PV // 2, tq)
    lam = (jnp.exp(jnp.sum(lq1_ref[...] * lk1_ref[...], axis=-1, keepdims=True))
           - jnp.exp(jnp.sum(lq2_ref[...] * lk2_ref[...], axis=-1, keepdims=True)) + lam_init)
    for t in range(tiles):
        rows = pl.ds(t * tq, tq)
        _stack_masked_heads(q_ref.at[rows, :], qm_ref, 1, tq, half)
        _softmax_pv(jnp.logical_and(pl.program_id(2) == 0, t == 0), qm_ref, k_ref, v_ref, s_ref, m_ref, vext_ref,
                    acc_ref)
        for j in range(tq // half):
            a = acc_ref[2 * j * half:(2 * j + 2) * half, :]
            o = a[:, 0:LANES] / a[:, LANES:2 * LANES]
            d = o[0:half] - lam * o[half:2 * half]
            o_ref[t * tq + j * half:t * tq + (j + 1) * half, :] = (
                _rmsnorm_rows(d, g_ref[...], SUBLN_EPS) * (1.0 - lam_init)).astype(BF16)


def _diff_attention(q, k, v, lq1, lk1, lq2, lk2, g, batch, seq, lam_init):
    tq = min(_attn_rows(seq) // 2, seq)
    rows = 2 * tq
    tiles = min(TILES_PER_STEP, seq // tq)
    nq = seq // (tq * tiles)
    vec = pl.BlockSpec((1, HEAD_DIM), lambda b, h, i: (0, 0))
    return pl.pallas_call(
        functools.partial(_diff_kernel, tq=tq, tiles=tiles, lam_init=lam_init),
        grid=(batch, DIFF_HEADS, nq),
        in_specs=[
            pl.BlockSpec((tq * tiles, LANES), lambda b, h, i: (b * nq + i, h)),
            pl.BlockSpec((seq, LANES), lambda b, h, i: (b, h)),
            pl.BlockSpec((seq, LANES), lambda b, h, i: (b, h)),
            vec, vec, vec, vec,
            pl.BlockSpec((1, LANES), lambda b, h, i: (0, 0)),
        ],
        out_specs=pl.BlockSpec((tq * tiles, LANES), lambda b, h, i: (b * nq + i, h)),
        out_shape=jax.ShapeDtypeStruct(q.shape, BF16),
        scratch_shapes=_attn_scratch(rows, seq),
        compiler_params=_params("parallel", "parallel", "arbitrary"),
        name="diff_attention",
    )(q, k, v, lq1, lk1, lq2, lk2, g)


def _na_band_start(n):
    return int(np.clip(n * NA_QCOLS - NA_WIN_COLS // 2, 0, GRID_W - NA_KCOLS))


def _na_bias_indices():
    qr = np.arange(NA_QROWS)[:, None]
    kr = np.arange(NA_KROWS)[None, :]
    half = NA_WIN_ROWS // 2
    key_shift = [0, -half, -(NA_KROWS - NA_QROWS)]
    win_start = [np.maximum(qr - half, 0), qr, np.minimum(qr + half, NA_KROWS - NA_WIN_ROWS)]
    drow = np.stack([np.clip(kr + shift - qr + NA_WIN_ROWS - 1, 0, 2 * NA_WIN_ROWS - 2) for shift in key_shift])
    rmask = np.stack([(kr >= w0) & (kr < w0 + NA_WIN_ROWS) for w0 in win_start])
    qc = np.arange(NA_QCOLS)[:, None]
    kc = np.arange(NA_KCOLS)[None, :]
    dcol, cmask = [], []
    for n in range(GRID_W // NA_QCOLS):
        qabs = n * NA_QCOLS + qc
        kabs = _na_band_start(n) + kc
        w0 = np.clip(qabs - NA_WIN_COLS // 2, 0, GRID_W - NA_WIN_COLS)
        dcol.append(np.clip(kabs - qabs + NA_WIN_COLS - 1, 0, 2 * NA_WIN_COLS - 2))
        cmask.append((kabs >= w0) & (kabs < w0 + NA_WIN_COLS))
    return drow, rmask, np.stack(dcol), np.stack(cmask)


def _na_bias_table(rpb):
    drow, rmask, dcol, cmask = _na_bias_indices()
    oh_r = jnp.asarray(np.eye(2 * NA_WIN_ROWS - 1, dtype=np.float32)[drow])
    oh_c = jnp.asarray(np.eye(2 * NA_WIN_COLS - 1, dtype=np.float32)[dcol])
    hp = lax.Precision.HIGHEST
    a = jnp.einsum("hab,vqka->hvqkb", rpb.astype(F32), oh_r, precision=hp)
    t = jnp.einsum("hvqkb,nxyb->vnhqxky", a, oh_c, precision=hp)
    mask = rmask[:, None, None, :, None, :, None] & cmask[None, :, None, None, :, None, :]
    t = jnp.where(jnp.asarray(mask), t * LOG2E, -jnp.inf)
    return t.reshape(t.shape[:3] + (NA_QROWS * NA_QCOLS, NA_KROWS * NA_KCOLS))


def _na_kernel(q_ref, k0, k1, k2, k3, v0, v1, v2, v3, tbl_ref, o_ref):
    kps = (k0, k1, k2, k3)
    vps = (v0, v1, v2, v3)
    lo = _lane_lo_mask()
    nq = NA_QROWS * NA_QCOLS
    for n in range(GRID_W // NA_QCOLS):
        band = _na_band_start(n)
        for c in range(A_WIDTH // LANES):
            lanes = slice(c * LANES, (c + 1) * LANES)
            qs = jnp.concatenate(
                [q_ref[r * GRID_W + n * NA_QCOLS:r * GRID_W + (n + 1) * NA_QCOLS, lanes] for r in range(NA_QROWS)],
                axis=0)
            zero = jnp.zeros_like(qs)
            qm = jnp.concatenate([jnp.where(lo, qs, zero), jnp.where(lo, zero, qs)], axis=0)

            def band_rows(pieces):
                return jnp.concatenate(
                    [p[r * GRID_W + band:r * GRID_W + band + NA_KCOLS, lanes]
                     for p in pieces for r in range(NA_KPIECE)], axis=0).astype(BF16)

            kb = band_rows(kps)
            vb = band_rows(vps)
            s = _dot_nt(qm, kb)
            s = s + jnp.concatenate([tbl_ref[0, n, 2 * c], tbl_ref[0, n, 2 * c + 1]], axis=0)
            m = jnp.max(s, axis=-1, keepdims=True)
            p = jnp.exp2(s - m)
            l = jnp.sum(p, axis=-1, keepdims=True)
            o = _dot(p.astype(BF16), vb) / l
            res = jnp.where(lo, o[0:nq], o[nq:2 * nq]).astype(BF16)
            for r in range(NA_QROWS):
                o_ref[r * GRID_W + n * NA_QCOLS:r * GRID_W + (n + 1) * NA_QCOLS, lanes] = (
                    res[r * NA_QCOLS:(r + 1) * NA_QCOLS])


def _na_attention(q, k, v, tbl, batch, seq):
    grid_rows = seq // GRID_W
    steps = grid_rows // NA_QROWS
    tq = NA_QROWS * GRID_W
    tp = NA_KPIECE * GRID_W
    pieces_per_batch = grid_rows // NA_KPIECE
    npieces = NA_KROWS // NA_KPIECE
    shift = (NA_WIN_ROWS // 2) // NA_KPIECE

    def piece_map(i):
        def index(b, t):
            start = jnp.clip(t * (NA_QROWS // NA_KPIECE) - shift, 0, pieces_per_batch - npieces)
            return (b * pieces_per_batch + start + i, 0)
        return index

    def tbl_map(b, t):
        variant = jnp.where(t == 0, 0, jnp.where(t == steps - 1, 2, 1))
        return (variant, 0, 0, 0, 0)

    piece_specs = [pl.BlockSpec((tp, A_WIDTH), piece_map(i)) for i in range(npieces)]
    return pl.pallas_call(
        _na_kernel,
        grid=(batch, steps),
        in_specs=[pl.BlockSpec((tq, A_WIDTH), lambda b, t: (b * steps + t, 0))] + piece_specs + piece_specs + [
            pl.BlockSpec((1,) + tbl.shape[1:], tbl_map)],
        out_specs=pl.BlockSpec((tq, A_WIDTH), lambda b, t: (b * steps + t, 0)),
        out_shape=jax.ShapeDtypeStruct(q.shape, BF16),
        compiler_params=_params("parallel", "arbitrary"),
        name="na_attention",
    )(q, k, k, k, k, v, v, v, v, tbl)


def _mlp_kernel(*refs, n_attn, final):
    x_ref = refs[0]
    attn_refs = refs[1:1 + n_attn]
    wo_ref, g_ref, wup_ref, wdown_ref, gf_ref, o_ref, h_ref, acc_ref = refs[1 + n_attn:]
    x = x_ref[...]
    width = wo_ref.shape[0] // n_attn
    for i, a_ref in enumerate(attn_refs):
        x = x + _dot(a_ref[...], wo_ref[i * width:(i + 1) * width, :])
    h = _rmsnorm_rows(x, g_ref[...], NORM_EPS).astype(BF16)
    h_ref[...] = h

    def hidden(hh, j):
        u = jnp.maximum(_dot(hh, wup_ref[j]), 0.0)
        return _dot((u * u).astype(BF16), wdown_ref[j])

    acc_ref[...] = x + hidden(h, 0)

    def hidden_chunk(j, carry):
        acc_ref[...] += hidden(h_ref[...], j)
        return carry

    last = wup_ref.shape[0] - 1
    lax.fori_loop(1, last, hidden_chunk, 0)
    y = acc_ref[...] + hidden(h_ref[...], last)
    if final:
        y = _rmsnorm_rows(y, gf_ref[...], NORM_EPS)
    o_ref[...] = y


def _outproj_mlp(x, attn, wo, g, wup, wdown, gf, final):
    n = x.shape[0]
    tm = min(TM_MLP, n)
    row = lambda i: (i, 0)
    const2 = lambda i: (0, 0)
    const3 = lambda i: (0, 0, 0)
    return pl.pallas_call(
        functools.partial(_mlp_kernel, n_attn=len(attn), final=final),
        grid=(n // tm,),
        in_specs=[pl.BlockSpec((tm, D_MODEL), row)]
        + [pl.BlockSpec((tm, a.shape[1]), row) for a in attn]
        + [
            pl.BlockSpec(wo.shape, const2),
            pl.BlockSpec((1, D_MODEL), const2),
            pl.BlockSpec(wup.shape, const3),
            pl.BlockSpec(wdown.shape, const3),
            pl.BlockSpec((1, D_MODEL), const2),
        ],
        out_specs=pl.BlockSpec((tm, D_MODEL), row),
        out_shape=jax.ShapeDtypeStruct(x.shape, F32),
        scratch_shapes=[pltpu.VMEM((tm, D_MODEL), BF16), pltpu.VMEM((tm, D_MODEL), F32)],
        compiler_params=_params("parallel"),
        name="outproj_mlp",
    )(x, *attn, wo, g, wup, wdown, gf)


def _rope_angles(pos, dim, theta):
    inv_freq = 1.0 / jnp.power(theta, jnp.arange(0, dim, 2, dtype=F32) / dim)
    ang = pos.astype(F32)[:, None] * inv_freq[None, :]
    return jnp.cos(ang), jnp.sin(ang)


def _axial_tables(seq):
    t = jnp.arange(seq)
    half = HEAD_DIM // 2
    cr, sr = _rope_angles(t // GRID_W, half, AXIAL_THETA)
    cc, sc = _rope_angles(t % GRID_W, half, AXIAL_THETA)
    cos = jnp.concatenate([cr, cr, cc, cc], axis=-1)
    sin = jnp.concatenate([-sr, sr, -sc, sc], axis=-1)
    return jnp.tile(cos, (1, LANES // HEAD_DIM)), jnp.tile(sin, (1, LANES // HEAD_DIM))


def _rope_tables(seq):
    c, s = _rope_angles(jnp.arange(seq), HEAD_DIM, ROPE_THETA)
    cos = jnp.concatenate([c, c], axis=-1)
    sin = jnp.concatenate([-s, s], axis=-1)
    return jnp.tile(cos, (1, LANES // HEAD_DIM)), jnp.tile(sin, (1, LANES // HEAD_DIM))


_GQA_HEAD_ORDER = np.arange(GQA_Q_HEADS).reshape(GQA_KV_HEADS, -1).T.reshape(-1)


def _prepare_even(w_in, q_norm, k_norm, w_out):
    a, bq = A_WIDTH, B_Q_WIDTH
    wqb = w_in[:, 3 * a:3 * a + bq].reshape(w_in.shape[0], GQA_Q_HEADS, HEAD_DIM)[:, _GQA_HEAD_ORDER, :]
    w = jnp.concatenate([w_in[:, :3 * a], wqb.reshape(w_in.shape[0], bq), w_in[:, 3 * a + bq:]], axis=1).astype(BF16)
    reps = LANES // HEAD_DIM
    gq = jnp.tile(q_norm, reps)[None]
    gk = jnp.tile(k_norm, reps)[None]
    wo_b = w_out[a:].reshape(GQA_Q_HEADS, HEAD_DIM, -1)[_GQA_HEAD_ORDER].reshape(bq, -1)
    wo = jnp.concatenate([w_out[:a], wo_b], axis=0).astype(BF16)
    return w, gq, gk, wo


def _group_ones():
    g = np.arange(LANES) // HEAD_DIM
    return jnp.asarray((g[:, None] == g[None, :]).astype(np.float32), dtype=BF16)


def _trunk(x3, p):
    batch, seq, _ = x3.shape
    assert seq % (NA_KROWS * GRID_W) == 0
    x = x3.reshape(batch * seq, D_MODEL)
    depth = len(p["layers"])
    for layer, lp in enumerate(p["layers"]):
        final = layer == depth - 1
        if layer % 2 == 0:
            cos, sin = _axial_tables(seq)
            qa, ka, va, qb, kb, vb = _inproj_even(x, lp["ln_mix"], lp["w_in"], p["gmat"], lp["gq"], lp["gk"],
                                                  cos, sin, seq)
            a_out = _na_attention(qa, ka, va, lp["na_bias"], batch, seq)
            b_out = _gqa_attention(qb, kb, vb, batch, seq)
            attn = [a_out, b_out]
        else:
            cos, sin = _rope_tables(seq)
            q, k, v = _inproj_odd(x, lp["ln_mix"], lp["w_in"], cos, sin, seq)
            attn = [_diff_attention(q, k, v, lp["lq1"], lp["lk1"], lp["lq2"], lp["lk2"], lp["subln"],
                                    batch, seq, lp["lam_init"])]
        x = _outproj_mlp(x, attn, lp["w_out"], lp["ln_mlp"], lp["w_up"], lp["w_down"], p["ln_f"], final)
    return x.reshape(batch, seq, D_MODEL)


def kernel(x_prompt, x_sample, ln_mix_e, w_in_e, rpb, q_norm_b, k_norm_b, w_out_e, ln_mix_o, w_in_o, lambda_q1, lambda_k1, lambda_q2, lambda_k2, subln_g, w_out_o, ln_mlp, w_up, w_down, ln_f):
    depth = ln_mlp.shape[0]
    layers = []
    for layer in range(depth):
        j = layer // 2
        lp = {
            "ln_mlp": ln_mlp[layer][None],
            "w_up": w_up[layer].astype(BF16).reshape(D_MODEL, D_FF // TF_MLP, TF_MLP).transpose(1, 0, 2),
            "w_down": w_down[layer].astype(BF16).reshape(D_FF // TF_MLP, TF_MLP, D_MODEL),
        }
        if layer % 2 == 0:
            w, gq, gk, wo = _prepare_even(w_in_e[j], q_norm_b[j], k_norm_b[j], w_out_e[j])
            lp.update(ln_mix=ln_mix_e[j][None], w_in=w, gq=gq, gk=gk, w_out=wo,
                      na_bias=_na_bias_table(rpb[j]))
        else:
            lp.update(ln_mix=ln_mix_o[j][None], w_in=w_in_o[j].astype(BF16), w_out=w_out_o[j].astype(BF16),
                      lq1=lambda_q1[j][None], lk1=lambda_k1[j][None], lq2=lambda_q2[j][None],
                      lk2=lambda_k2[j][None], subln=subln_g[j][None],
                      lam_init=0.8 - 0.6 * math.exp(-0.3 * layer))
        layers.append(lp)
    p = {"layers": layers, "gmat": _group_ones(), "ln_f": ln_f[None]}
    return (_trunk(x_prompt, p), _trunk(x_sample, p))
```
